```python
import math
import jax, jax.numpy as jnp
from jax import lax
import numpy as np

D_MODEL = 1024
BATCH = 16
SEQ = 2048
DEPTH = 4

GRID_W = 64
CTX_LEN = 256
N_MIXERS = 3
N_LAYERS_A = (DEPTH + 2) // 3
N_LAYERS_B = (DEPTH + 1) // 3
N_LAYERS_C = DEPTH // 3
MLP_HIDDEN = 4 * D_MODEL
NORM_EPS = 1e-6
ROPE_THETA = 10000.0
Q_BLOCK = 128
N_MOD = 6
DA_HEAD_DIM = 64
DA_HEADS = D_MODEL // (2 * DA_HEAD_DIM)
DA_SCALE = DA_HEAD_DIM ** -0.5
MLA_HEADS = 16
MLA_Q_LORA = 256
MLA_KV_LORA = 128
MLA_NOPE = 64
MLA_ROPE = 32
MLA_V = 64
MLA_SCALE = (MLA_NOPE + MLA_ROPE) ** -0.5
SC_WIDTH = 3

kernel_name = "hybrid_diffattn_mla_shortconv_dit"


def rms_norm(x, g):
    xf = x.astype(jnp.float32)
    y = xf * lax.rsqrt(jnp.mean(xf * xf, axis=-1, keepdims=True) + NORM_EPS)
    return (y * g.astype(jnp.float32)).astype(x.dtype)


def modulate(x, shift, scale):
    return x * (1 + scale) + shift


def axial_rope_tables(row, col, rot_dim):
    quarter = rot_dim // 4
    inv_freq = ROPE_THETA ** (-jnp.arange(quarter, dtype=jnp.float32) / quarter)
    ang = jnp.concatenate([row[:, None] * inv_freq, col[:, None] * inv_freq], axis=-1)
    return jnp.cos(ang), jnp.sin(ang)


def apply_rope(x, cos, sin):
    half = x.shape[-1] // 2
    xf = x.astype(jnp.float32)
    x1, x2 = xf[..., :half], xf[..., half:]
    return jnp.concatenate([x1 * cos - x2 * sin, x2 * cos + x1 * sin], axis=-1).astype(x.dtype)


def sweep_query_blocks(fn, qs):
    b, s = qs[0].shape[:2]
    nblk = s // Q_BLOCK
    blocks = tuple(jnp.moveaxis(q.reshape((b, nblk, Q_BLOCK) + q.shape[2:]), 1, 0) for q in qs)
    out = lax.map(lambda qb: fn(*qb), blocks)
    return jnp.moveaxis(out, 0, 1).reshape(b, s, -1)


def _diff_attend(q, k, v, lam, subln_g, lambda_init):
    b, nq = q.shape[0], q.shape[1]
    s = jnp.einsum('bqhd,bkhd->bhqk', q, k, preferred_element_type=jnp.float32) * DA_SCALE
    p = jax.nn.softmax(s, axis=-1).reshape(b, DA_HEADS, 2, nq, -1)
    a = p[:, :, 0] - lam * p[:, :, 1]
    o = jnp.einsum('bhqk,bkhe->bqhe', a.astype(v.dtype), v)
    o = rms_norm(o, subln_g) * (1.0 - lambda_init)
    return o.reshape(b, nq, DA_HEADS * 2 * DA_HEAD_DIM)


def diff_attention(a_lat, a_ctx, w_qkv, lam_vecs, subln_g, w_out, lambda_init, cos, sin, ctx_out):
    def project(a):
        b, n, _ = a.shape
        q, k, v = jnp.split(a @ w_qkv, 3, axis=-1)
        return (q.reshape(b, n, 2 * DA_HEADS, DA_HEAD_DIM),
                k.reshape(b, n, 2 * DA_HEADS, DA_HEAD_DIM),
                v.reshape(b, n, DA_HEADS, 2 * DA_HEAD_DIM))

    q_l, k_l, v_l = project(a_lat)
    q_c, k_c, v_c = project(a_ctx)
    cs, sn = cos[:, None, :], sin[:, None, :]
    q_l = apply_rope(q_l, cs, sn)
    k_l = apply_rope(k_l, cs, sn)
    lv = lam_vecs.astype(jnp.float32)
    lam = jnp.exp(jnp.sum(lv[0] * lv[1])) - jnp.exp(jnp.sum(lv[2] * lv[3])) + lambda_init
    k_all = jnp.concatenate([k_c, k_l], axis=1)
    v_all = jnp.concatenate([v_c, v_l], axis=1)
    y_lat = sweep_query_blocks(
        lambda qb: _diff_attend(qb, k_all, v_all, lam, subln_g, lambda_init), (q_l,)) @ w_out
    y_ctx = _diff_attend(q_c, k_c, v_c, lam, subln_g, lambda_init) @ w_out if ctx_out else None
    return y_lat, y_ctx


def _mla_attend(qn, qr, kn, kr, v):
    s = (jnp.einsum('bqhd,bkhd->bhqk', qn, kn, preferred_element_type=jnp.float32)
         + jnp.einsum('bqhd,bkd->bhqk', qr, kr, preferred_element_type=jnp.float32)) * MLA_SCALE
    p = jax.nn.softmax(s, axis=-1).astype(v.dtype)
    o = jnp.einsum('bhqk,bkhe->bqhe', p, v)
    return o.reshape(o.shape[0], o.shape[1], MLA_HEADS * MLA_V)


def mla_attention(a_lat, a_ctx, w_down, q_norm_g, w_uq, kv_norm_g, w_ukv, w_out, cos, sin, ctx_out):
    def project(a):
        b, n, _ = a.shape
        cq, ckv, kr = jnp.split(a @ w_down, [MLA_Q_LORA, MLA_Q_LORA + MLA_KV_LORA], axis=-1)
        q = (rms_norm(cq, q_norm_g) @ w_uq).reshape(b, n, MLA_HEADS, MLA_NOPE + MLA_ROPE)
        kv = (rms_norm(ckv, kv_norm_g) @ w_ukv).reshape(b, n, MLA_HEADS, MLA_NOPE + MLA_V)
        return q[..., :MLA_NOPE], q[..., MLA_NOPE:], kv[..., :MLA_NOPE], kr, kv[..., MLA_NOPE:]

    qn_l, qr_l, kn_l, kr_l, v_l = project(a_lat)
    qn_c, qr_c, kn_c, kr_c, v_c = project(a_ctx)
    qr_l = apply_rope(qr_l, cos[:, None, :], sin[:, None, :])
    kr_l = apply_rope(kr_l, cos, sin)
    kn_all = jnp.concatenate([kn_c, kn_l], axis=1)
    kr_all = jnp.concatenate([kr_c, kr_l], axis=1)
    v_all = jnp.concatenate([v_c, v_l], axis=1)
    y_lat = sweep_query_blocks(
        lambda qn_b, qr_b: _mla_attend(qn_b, qr_b, kn_all, kr_all, v_all), (qn_l, qr_l)) @ w_out
    y_ctx = _mla_attend(qn_c, qr_c, kn_c, kr_c, v_c) @ w_out if ctx_out else None
    return y_lat, y_ctx


def short_conv(a, w_in, conv_w, w_out):
    d = a.shape[-1]
    b_gate, c_gate, h = jnp.split(a @ w_in, 3, axis=-1)
    u = lax.conv_general_dilated(c_gate * h, conv_w[:, None, :].astype(a.dtype),
                                 window_strides=(1,), padding=((SC_WIDTH // 2, SC_WIDTH // 2),),
                                 dimension_numbers=('NWC', 'WIO', 'NWC'), feature_group_count=d)
    return (b_gate * u) @ w_out


def squared_relu_mlp(a, w_in, w_out):
    return jnp.square(jax.nn.relu(a @ w_in)) @ w_out


def setup_inputs(seed: int = 0) -> dict:
    key = jax.random.key(seed)
    ks = iter(jax.random.split(key, 32))
    D = D_MODEL

    def nrm(shape, scale=1.0):
        return jax.random.normal(next(ks), shape, jnp.float32) * scale

    def w(shape, fan_in, scale=1.0):
        return nrm(shape, scale * fan_in ** -0.5)

    def gain(shape):
        return 1.0 + nrm(shape, 0.02)

    return {
        "x": nrm((BATCH, SEQ, D)),
        "c": nrm((BATCH, D)),
        "ctx": nrm((BATCH, CTX_LEN, D)),
        "c_ctx": nrm((D,)),
        "w_ada": w((DEPTH, D, N_MOD * D), D, 0.5),
        "b_ada": nrm((DEPTH, N_MOD * D), 0.01),
        "norm_g": gain((DEPTH, 4, D)),
        "w_mlp_in": w((DEPTH, D, MLP_HIDDEN), D),
        "w_mlp_out": w((DEPTH, MLP_HIDDEN, D), MLP_HIDDEN),
        "w_da_qkv": w((N_LAYERS_A, D, 3 * 2 * DA_HEADS * DA_HEAD_DIM), D),
        "da_lambda": nrm((N_LAYERS_A, 4, DA_HEAD_DIM), 0.1),
        "da_subln": gain((N_LAYERS_A, 2 * DA_HEAD_DIM)),
        "w_da_out": w((N_LAYERS_A, 2 * DA_HEADS * DA_HEAD_DIM, D), 2 * DA_HEADS * DA_HEAD_DIM),
        "w_mla_down": w((N_LAYERS_B, D, MLA_Q_LORA + MLA_KV_LORA + MLA_ROPE), D),
        "mla_q_norm": gain((N_LAYERS_B, MLA_Q_LORA)),
        "w_mla_uq": w((N_LAYERS_B, MLA_Q_LORA, MLA_HEADS * (MLA_NOPE + MLA_ROPE)), MLA_Q_LORA),
        "mla_kv_norm": gain((N_LAYERS_B, MLA_KV_LORA)),
        "w_mla_ukv": w((N_LAYERS_B, MLA_KV_LORA, MLA_HEADS * (MLA_NOPE + MLA_V)), MLA_KV_LORA),
        "w_mla_out": w((N_LAYERS_B, MLA_HEADS * MLA_V, D), MLA_HEADS * MLA_V),
        "w_sc_in": w((N_LAYERS_C, D, 3 * D), D),
        "sc_conv": w((N_LAYERS_C, SC_WIDTH, D), SC_WIDTH),
        "w_sc_out": w((N_LAYERS_C, D, D), D),
    }


def reference(x, c, ctx, c_ctx, w_ada, b_ada, norm_g, w_mlp_in, w_mlp_out,
              w_da_qkv, da_lambda, da_subln, w_da_out,
              w_mla_down, mla_q_norm, w_mla_uq, mla_kv_norm, w_mla_ukv, w_mla_out,
              w_sc_in, sc_conv, w_sc_out):
    n_lat = x.shape[1]
    rows = n_lat // GRID_W
    row = jnp.repeat(jnp.arange(rows, dtype=jnp.float32), GRID_W)
    col = jnp.tile(jnp.arange(GRID_W, dtype=jnp.float32), rows)
    cos_da, sin_da = axial_rope_tables(row, col, DA_HEAD_DIM)
    cos_mla, sin_mla = axial_rope_tables(row, col, MLA_ROPE)

    silu_c = jax.nn.silu(c)
    silu_cc = jax.nn.silu(c_ctx)
    h_lat, h_ctx = x, ctx
    for i in range(DEPTH):
        kind, j = i % N_MIXERS, i // N_MIXERS
        ctx_out = i < DEPTH - 1
        g = norm_g[i]
        m_lat = jnp.split((silu_c @ w_ada[i] + b_ada[i])[:, None, :], N_MOD, axis=-1)
        m_ctx = jnp.split(silu_cc @ w_ada[i] + b_ada[i], N_MOD, axis=-1)

        a_lat = modulate(rms_norm(h_lat, g[0]), m_lat[0], m_lat[1])
        if kind == 0:
            a_ctx = modulate(rms_norm(h_ctx, g[0]), m_ctx[0], m_ctx[1])
            lambda_init = 0.8 - 0.6 * math.exp(-0.3 * i)
            y_lat, y_ctx = diff_attention(a_lat, a_ctx, w_da_qkv[j], da_lambda[j], da_subln[j],
                                          w_da_out[j], lambda_init, cos_da, sin_da, ctx_out)
        elif kind == 1:
            a_ctx = modulate(rms_norm(h_ctx, g[0]), m_ctx[0], m_ctx[1])
            y_lat, y_ctx = mla_attention(a_lat, a_ctx, w_mla_down[j], mla_q_norm[j], w_mla_uq[j],
                                         mla_kv_norm[j], w_mla_ukv[j], w_mla_out[j],
                                         cos_mla, sin_mla, ctx_out)
        else:
            y_lat = short_conv(a_lat, w_sc_in[j], sc_conv[j], w_sc_out[j])
            if ctx_out:
                a_ctx = modulate(rms_norm(h_ctx, g[0]), m_ctx[0], m_ctx[1])
                y_ctx = short_conv(a_ctx, w_sc_in[j], sc_conv[j], w_sc_out[j])

        h_lat = h_lat + m_lat[2] * rms_norm(y_lat, g[1])
        f_lat = squared_relu_mlp(modulate(rms_norm(h_lat, g[2]), m_lat[3], m_lat[4]), w_mlp_in[i], w_mlp_out[i])
        h_lat = h_lat + m_lat[5] * rms_norm(f_lat, g[3])
        if ctx_out:
            h_ctx = h_ctx + m_ctx[2] * rms_norm(y_ctx, g[1])
            f_ctx = squared_relu_mlp(modulate(rms_norm(h_ctx, g[2]), m_ctx[3], m_ctx[4]), w_mlp_in[i], w_mlp_out[i])
            h_ctx = h_ctx + m_ctx[5] * rms_norm(f_ctx, g[3])
    return h_lat
```

```python
import functools
import math

import jax
import jax.numpy as jnp
from jax import lax
from jax.experimental import pallas as pl
from jax.experimental.pallas import tpu as pltpu

F32 = jnp.float32
BF16 = jnp.bfloat16

LANES = 128
SUBLANES = 8
VMEM_LIMIT_BYTES = 56 * 1024 * 1024

D = 1024
CTX = 256
SEQ = 2048
NT = CTX + SEQ
DEPTH = 4
N_MOD = 6
MLP_HIDDEN = 4 * D
NORM_EPS = 1e-6
ROPE_THETA = 10000.0
GRID_W = 64

DA_HEAD_DIM = 64
DA_HEADS = 8
DA_SCALE = DA_HEAD_DIM ** -0.5

MLA_HEADS = 16
MLA_Q_LORA = 256
MLA_KV_LORA = 128
MLA_NOPE = 64
MLA_ROPE = 32
MLA_V = 64
MLA_SCALE = (MLA_NOPE + MLA_ROPE) ** -0.5

SC_WIDTH = 3

TM = CTX
NTILES = NT // TM
MLP_CHUNK = 1024
MOD_ROWS = 24


def _params(n_axes):
    return pltpu.CompilerParams(
        dimension_semantics=("arbitrary",) * n_axes,
        vmem_limit_bytes=VMEM_LIMIT_BYTES,
    )


def _rms(x, g):
    ms = jnp.mean(x * x, axis=-1, keepdims=True)
    return x * lax.rsqrt(ms + NORM_EPS) * g


def _modnorm(h, g, mod):
    return _rms(h, g) * (1.0 + mod[1:2, :]) + mod[0:1, :]


def _rope_block(x, rope_ref, sh_hi, sh_lo):
    return (x * rope_ref[0]
            + pltpu.roll(x, sh_hi, 1) * rope_ref[1]
            + pltpu.roll(x, sh_lo, 1) * rope_ref[2])


def _ada_kernel(c_ref, w_ref, b_ref, o_ref):
    c = c_ref[...]
    s = c / (1.0 + jnp.exp(-c))
    o_ref[...] = jnp.dot(s.astype(BF16), w_ref[...].astype(BF16),
                         preferred_element_type=F32) + b_ref[...]


def _ada_all_layers(cc, w_ada, b_ada):
    return pl.pallas_call(
        _ada_kernel,
        grid=(DEPTH, N_MOD),
        in_specs=[
            pl.BlockSpec((MOD_ROWS, D), lambda l, n: (0, 0)),
            pl.BlockSpec((None, D, D), lambda l, n: (l, 0, n)),
            pl.BlockSpec((None, 1, D), lambda l, n: (l, 0, n)),
        ],
        out_specs=pl.BlockSpec((None, MOD_ROWS, D), lambda l, n: (l, 0, n)),
        out_shape=jax.ShapeDtypeStruct((DEPTH, MOD_ROWS, N_MOD * D), F32),
        compiler_params=_params(2),
        name="ada_mod",
    )(cc, w_ada, b_ada.reshape(DEPTH, 1, N_MOD * D))


def _mod_spec(layer, batch, first_tile):
    def index(b, t):
        row = jnp.where(t + first_tile == 0, batch, b)
        return (layer, row, 0, 0)
    return pl.BlockSpec((None, None, N_MOD, D), index)


def _tile_spec(width, first_tile=0):
    return pl.BlockSpec((None, TM, width), lambda b, t: (b, t + first_tile, 0))


def _full_spec(shape):
    nd = len(shape)
    return pl.BlockSpec(shape, lambda b, t: (0,) * nd)


def _da_pre_kernel(h_ref, mod_ref, g_ref, w_ref, rope_ref, q_ref, k_ref, v_ref):
    a = _modnorm(h_ref[...], g_ref[0:1, :], mod_ref[...])
    qkv = jnp.dot(a.astype(BF16), w_ref[...], preferred_element_type=F32)
    for j in range(D // LANES):
        lo, hi = j * LANES, (j + 1) * LANES
        q = _rope_block(qkv[:, lo:hi], rope_ref, LANES - 32, 32)
        q_ref[:, lo:hi] = (q * DA_SCALE).astype(BF16)
        k = _rope_block(qkv[:, D + lo:D + hi], rope_ref, LANES - 32, 32)
        k_ref[:, lo:hi] = k.astype(BF16)
    v_ref[...] = qkv[:, 2 * D:].astype(BF16)


def _da_pre(h, mods, layer, g, w_qkv, rope):
    b = h.shape[0]
    out = jax.ShapeDtypeStruct((b, NT, D), BF16)
    return pl.pallas_call(
        _da_pre_kernel,
        grid=(b, NTILES),
        in_specs=[
            _tile_spec(D),
            _mod_spec(layer, b, 0),
            _full_spec((4, D)),
            _full_spec((D, 3 * D)),
            pl.BlockSpec((3, TM, LANES), lambda b_, t: (0, t, 0)),
        ],
        out_specs=[_tile_spec(D)] * 3,
        out_shape=[out] * 3,
        compiler_params=_params(2),
        name="da_pre",
    )(h, mods, g, w_qkv, rope)


def _da_attn_kernel(lam_ref, sub_ref, q_ref, k_ref, v_ref, o_ref, *, lambda_init, first_tile):
    lv = lam_ref[...]
    lam = (jnp.exp(jnp.sum(lv[0:1] * lv[1:2], axis=-1, keepdims=True))
           - jnp.exp(jnp.sum(lv[2:3] * lv[3:4], axis=-1, keepdims=True))
           + lambda_init)

    def attend(nk):
        q = q_ref[...]
        lane = lax.broadcasted_iota(jnp.int32, q.shape, 1)
        zero = jnp.zeros_like(q)
        qq = jnp.concatenate([jnp.where(lane < DA_HEAD_DIM, q, zero),
                              jnp.where(lane >= DA_HEAD_DIM, q, zero)], axis=0)
        k = k_ref[0:nk, :]
        v = v_ref[0:nk, :]
        s = lax.dot_general(qq, k, (((1,), (1,)), ((), ())), preferred_element_type=F32)
        e = jnp.exp(s - jnp.max(s, axis=-1, keepdims=True))
        r = 1.0 / jnp.sum(e, axis=-1, keepdims=True)
        a = e[:TM] * r[:TM] - e[TM:] * (lam * r[TM:])
        o = jnp.dot(a.astype(BF16), v, preferred_element_type=F32)
        o_ref[...] = (_rms(o, sub_ref[...]) * (1.0 - lambda_init)).astype(BF16)

    if first_tile == 0:
        t = pl.program_id(2)
        pl.when(t == 0)(lambda: attend(CTX))
        pl.when(t > 0)(lambda: attend(NT))
    else:
        attend(NT)


def _da_attn(q, k, v, lam_vecs, subln, lambda_init, ctx_out):
    b = q.shape[0]
    first_tile = 0 if ctx_out else 1
    kv_spec = pl.BlockSpec((None, NT, LANES), lambda b_, h, t: (b_, 0, h))
    qo_spec = pl.BlockSpec((None, TM, LANES), lambda b_, h, t: (b_, t + first_tile, h))
    return pl.pallas_call(
        functools.partial(_da_attn_kernel, lambda_init=lambda_init, first_tile=first_tile),
        grid=(b, DA_HEADS, NTILES - first_tile),
        in_specs=[
            pl.BlockSpec((4, DA_HEAD_DIM), lambda b_, h, t: (0, 0)),
            pl.BlockSpec((1, LANES), lambda b_, h, t: (0, 0)),
            qo_spec, kv_spec, kv_spec,
        ],
        out_specs=qo_spec,
        out_shape=jax.ShapeDtypeStruct((b, NT, D), BF16),
        compiler_params=_params(3),
        name="da_attn",
    )(lam_vecs, subln.reshape(1, LANES), q, k, v)


MLA_DOWN_W = 512
MLA_Q_W = MLA_HEADS * LANES
MLA_KV_W = MLA_HEADS * LANES + MLA_HEADS * MLA_V


def _mla_pre_kernel(h_ref, mod_ref, g_ref, wd_ref, qg_ref, wq_ref, kg_ref, wkv_ref, rope_ref,
                    q_ref, k_ref, v_ref):
    a = _modnorm(h_ref[...], g_ref[0:1, :], mod_ref[...])
    d = jnp.dot(a.astype(BF16), wd_ref[...], preferred_element_type=F32)
    cq = _rms(d[:, :MLA_Q_LORA], qg_ref[...])
    ckv = _rms(d[:, MLA_Q_LORA:MLA_Q_LORA + MLA_KV_LORA], kg_ref[...])
    kr = _rope_block(d[:, MLA_Q_LORA + MLA_KV_LORA:], rope_ref, LANES - 16, 16)
    q = jnp.dot(cq.astype(BF16), wq_ref[...], preferred_element_type=F32)
    kv = jnp.dot(ckv.astype(BF16), wkv_ref[...], preferred_element_type=F32)
    for j in range(MLA_HEADS):
        lo, hi = j * LANES, (j + 1) * LANES
        qj = _rope_block(q[:, lo:hi], rope_ref, LANES - 16, 16)
        q_ref[:, lo:hi] = (qj * MLA_SCALE).astype(BF16)
        k_ref[:, lo:hi] = (kv[:, lo:hi] + kr).astype(BF16)
    v_ref[...] = kv[:, MLA_HEADS * LANES:].astype(BF16)


def _mla_pre(h, mods, layer, g, wd, qg, wq, kg, wkv, rope):
    b = h.shape[0]
    return pl.pallas_call(
        _mla_pre_kernel,
        grid=(b, NTILES),
        in_specs=[
            _tile_spec(D),
            _mod_spec(layer, b, 0),
            _full_spec((4, D)),
            _full_spec((D, MLA_DOWN_W)),
            _full_spec((1, MLA_Q_LORA)),
            _full_spec((MLA_Q_LORA, MLA_Q_W)),
            _full_spec((1, MLA_KV_LORA)),
            _full_spec((MLA_KV_LORA, MLA_KV_W)),
            pl.BlockSpec((3, TM, LANES), lambda b_, t: (0, t, 0)),
        ],
        out_specs=[_tile_spec(MLA_Q_W), _tile_spec(MLA_Q_W), _tile_spec(D)],
        out_shape=[jax.ShapeDtypeStruct((b, NT, MLA_Q_W), BF16),
                   jax.ShapeDtypeStruct((b, NT, MLA_Q_W), BF16),
                   jax.ShapeDtypeStruct((b, NT, D), BF16)],
        compiler_params=_params(2),
        name="mla_pre",
    )(h, mods, g, wd, qg, wq, kg, wkv, rope)


def _mla_attn_kernel(q_ref, k_ref, v_ref, o_ref, *, first_tile):
    def attend(nk):
        v = v_ref[0:nk, :]
        outs = []
        for i in range(2):
            q = q_ref[:, i * LANES:(i + 1) * LANES]
            k = k_ref[0:nk, i * LANES:(i + 1) * LANES]
            s = lax.dot_general(q, k, (((1,), (1,)), ((), ())), preferred_element_type=F32)
            e = jnp.exp(s - jnp.max(s, axis=-1, keepdims=True))
            r = 1.0 / jnp.sum(e, axis=-1, keepdims=True)
            outs.append(jnp.dot(e.astype(BF16), v, preferred_element_type=F32) * r)
        lane = lax.broadcasted_iota(jnp.int32, outs[0].shape, 1)
        o_ref[...] = jnp.where(lane < MLA_V, outs[0], outs[1]).astype(BF16)

    if first_tile == 0:
        t = pl.program_id(2)
        pl.when(t == 0)(lambda: attend(CTX))
        pl.when(t > 0)(lambda: attend(NT))
    else:
        attend(NT)


def _mla_attn(q, k, v, ctx_out):
    b = q.shape[0]
    first_tile = 0 if ctx_out else 1
    pairs = MLA_HEADS // 2
    return pl.pallas_call(
        functools.partial(_mla_attn_kernel, first_tile=first_tile),
        grid=(b, pairs, NTILES - first_tile),
        in_specs=[
            pl.BlockSpec((None, TM, 2 * LANES), lambda b_, h, t: (b_, t + first_tile, h)),
            pl.BlockSpec((None, NT, 2 * LANES), lambda b_, h, t: (b_, 0, h)),
            pl.BlockSpec((None, NT, LANES), lambda b_, h, t: (b_, 0, h)),
        ],
        out_specs=pl.BlockSpec((None, TM, LANES), lambda b_, h, t: (b_, t + first_tile, h)),
        out_shape=jax.ShapeDtypeStruct((b, NT, D), BF16),
        compiler_params=_params(3),
        name="mla_attn",
    )(q, k, v)


def _sc_pre_kernel(hp_ref, h_ref, hn_ref, mod_ref, g_ref, w_ref, cw_ref, y_ref):
    t = pl.program_id(1)
    rows = jnp.concatenate([hp_ref[...], h_ref[...], hn_ref[...]], axis=0)
    a = _modnorm(rows, g_ref[0:1, :], mod_ref[...])
    p = jnp.dot(a.astype(BF16), w_ref[...], preferred_element_type=F32)
    z = p[:, D:2 * D] * p[:, 2 * D:]
    b_gate = p[SUBLANES:SUBLANES + TM, :D]
    row = lax.broadcasted_iota(jnp.int32, (TM, 1), 0)
    first = jnp.logical_and(row == 0, t <= 1)
    last = jnp.logical_and(row == TM - 1, jnp.logical_or(t == 0, t == NTILES - 1))
    z_prev = jnp.where(first, 0.0, z[SUBLANES - 1:SUBLANES - 1 + TM])
    z_next = jnp.where(last, 0.0, z[SUBLANES + 1:SUBLANES + 1 + TM])
    u = cw_ref[0:1, :] * z_prev + cw_ref[1:2, :] * z[SUBLANES:SUBLANES + TM] + cw_ref[2:3, :] * z_next
    y_ref[...] = (b_gate * u).astype(BF16)


def _sc_pre(h, mods, layer, g, w_in, conv_w):
    b = h.shape[0]
    per_tile = TM // SUBLANES
    last_halo = NT // SUBLANES - 1
    return pl.pallas_call(
        _sc_pre_kernel,
        grid=(b, NTILES),
        in_specs=[
            pl.BlockSpec((None, SUBLANES, D),
                         lambda b_, t: (b_, jnp.maximum(t * per_tile - 1, 0), 0)),
            _tile_spec(D),
            pl.BlockSpec((None, SUBLANES, D),
                         lambda b_, t: (b_, jnp.minimum((t + 1) * per_tile, last_halo), 0)),
            _mod_spec(layer, b, 0),
            _full_spec((4, D)),
            _full_spec((D, 3 * D)),
            _full_spec((SC_WIDTH, D)),
        ],
        out_specs=_tile_spec(D),
        out_shape=jax.ShapeDtypeStruct((b, NT, D), BF16),
        compiler_params=_params(2),
        name="sc_pre",
    )(h, h, h, mods, g, w_in, conv_w)


def _post_kernel(h_ref, y_ref, mod_ref, g_ref, wo_ref, w1_ref, w2_ref, o_ref):
    mod = mod_ref[...]
    y = jnp.dot(y_ref[...], wo_ref[...], preferred_element_type=F32)
    h = h_ref[...] + mod[2:3, :] * _rms(y, g_ref[1:2, :])
    a = (_rms(h, g_ref[2:3, :]) * (1.0 + mod[4:5, :]) + mod[3:4, :]).astype(BF16)
    f = jnp.zeros((TM, D), F32)
    for c in range(MLP_HIDDEN // MLP_CHUNK):
        lo, hi = c * MLP_CHUNK, (c + 1) * MLP_CHUNK
        u = jnp.maximum(jnp.dot(a, w1_ref[:, lo:hi], preferred_element_type=F32), 0.0)
        f = f + jnp.dot((u * u).astype(BF16), w2_ref[lo:hi, :], preferred_element_type=F32)
    o_ref[...] = h + mod[5:6, :] * _rms(f, g_ref[3:4, :])


def _post(h, y, mods, layer, g, w_out, w1, w2, ctx_out):
    b = h.shape[0]
    first_tile = 0 if ctx_out else 1
    n_out = NT if ctx_out else SEQ
    return pl.pallas_call(
        _post_kernel,
        grid=(b, NTILES - first_tile),
        in_specs=[
            _tile_spec(D, first_tile),
            _tile_spec(D, first_tile),
            _mod_spec(layer, b, first_tile),
            _full_spec((4, D)),
            _full_spec((D, D)),
            _full_spec((D, MLP_HIDDEN)),
            _full_spec((MLP_HIDDEN, D)),
        ],
        out_specs=_tile_spec(D),
        out_shape=jax.ShapeDtypeStruct((b, n_out, D), F32),
        compiler_params=_params(2),
        name="post_mlp",
    )(h, y, mods, g, w_out, w1, w2)


def _rope_tables(rot_dim, lane0, chunk):
    quarter = rot_dim // 4
    half = rot_dim // 2
    pos = jnp.arange(SEQ, dtype=F32)
    row = jnp.floor(pos / GRID_W)
    col = pos - row * GRID_W
    inv_freq = ROPE_THETA ** (-jnp.arange(quarter, dtype=F32) / quarter)
    ang = jnp.concatenate([row[:, None] * inv_freq, col[:, None] * inv_freq], axis=-1)
    cos, sin = jnp.cos(ang), jnp.sin(ang)
    lane = jnp.arange(LANES)
    off = (lane % chunk) - lane0
    in_rot = (off >= 0) & (off < rot_dim)
    first = in_rot & (off < half)
    second = in_rot & (off >= half)
    idx = jnp.clip(off % half, 0, half - 1)
    c_tab = jnp.where(in_rot[None, :], cos[:, idx], 1.0)
    s_hi = jnp.where(first[None, :], -sin[:, idx], 0.0)
    s_lo = jnp.where(second[None, :], sin[:, idx], 0.0)
    lat = jnp.stack([c_tab, s_hi, s_lo])
    ctx = jnp.stack([jnp.ones((CTX, LANES), F32), jnp.zeros((CTX, LANES), F32),
                     jnp.zeros((CTX, LANES), F32)])
    return jnp.concatenate([ctx, lat], axis=1)


def _mla_weight_layouts(w_down, w_uq, w_ukv):
    zeros = functools.partial(jnp.zeros, dtype=w_down.dtype)
    n_lat = MLA_Q_LORA + MLA_KV_LORA
    wd = jnp.concatenate([w_down[:, :n_lat], zeros((D, MLA_NOPE)), w_down[:, n_lat:],
                          zeros((D, LANES - MLA_NOPE - MLA_ROPE))], axis=1)
    wq = w_uq.reshape(MLA_Q_LORA, MLA_HEADS, MLA_NOPE + MLA_ROPE)
    wq = jnp.pad(wq, ((0, 0), (0, 0), (0, LANES - MLA_NOPE - MLA_ROPE))).reshape(MLA_Q_LORA, MLA_Q_W)
    wkv = w_ukv.reshape(MLA_KV_LORA, MLA_HEADS, MLA_NOPE + MLA_V)
    wk = jnp.pad(wkv[..., :MLA_NOPE], ((0, 0), (0, 0), (0, LANES - MLA_NOPE)))
    wkv = jnp.concatenate([wk.reshape(MLA_KV_LORA, MLA_HEADS * LANES),
                           wkv[..., MLA_NOPE:].reshape(MLA_KV_LORA, MLA_HEADS * MLA_V)], axis=1)
    return wd.astype(BF16), wq.astype(BF16), wkv.astype(BF16)


def kernel(x, c, ctx, c_ctx, w_ada, b_ada, norm_g, w_mlp_in, w_mlp_out, w_da_qkv, da_lambda, da_subln, w_da_out, w_mla_down, mla_q_norm, w_mla_uq, mla_kv_norm, w_mla_ukv, w_mla_out, w_sc_in, sc_conv, w_sc_out):
    b = x.shape[0]
    assert x.shape == (b, SEQ, D) and ctx.shape == (b, CTX, D) and b + 1 <= MOD_ROWS

    cc = jnp.concatenate([c, c_ctx[None, :], jnp.zeros((MOD_ROWS - b - 1, D), F32)], axis=0)
    mods = _ada_all_layers(cc, w_ada, b_ada).reshape(DEPTH, MOD_ROWS, N_MOD, D)

    rope_da = _rope_tables(DA_HEAD_DIM, 0, DA_HEAD_DIM)
    rope_mla = _rope_tables(MLA_ROPE, MLA_NOPE, LANES)

    h = jnp.concatenate([ctx, x], axis=1)
    for i in range(DEPTH):
        kind, j = i % 3, i // 3
        ctx_out = i < DEPTH - 1
        g = norm_g[i]
        if kind == 0:
            lambda_init = 0.8 - 0.6 * math.exp(-0.3 * i)
            q, k, v = _da_pre(h, mods, i, g, w_da_qkv[j].astype(BF16), rope_da)
            y = _da_attn(q, k, v, da_lambda[j], da_subln[j], lambda_init, ctx_out)
            w_out = w_da_out[j]
        elif kind == 1:
            wd, wq, wkv = _mla_weight_layouts(w_mla_down[j], w_mla_uq[j], w_mla_ukv[j])
            q, k, v = _mla_pre(h, mods, i, g, wd, mla_q_norm[j].reshape(1, -1), wq,
                               mla_kv_norm[j].reshape(1, -1), wkv, rope_mla)
            y = _mla_attn(q, k, v, ctx_out)
            w_out = w_mla_out[j]
        else:
            y = _sc_pre(h, mods, i, g, w_sc_in[j].astype(BF16), sc_conv[j])
            w_out = w_sc_out[j]
        h = _post(h, y, mods, i, g, w_out.astype(BF16), w_mlp_in[i].astype(BF16),
                  w_mlp_out[i].astype(BF16), ctx_out)
    return h
```

```python
import functools
import math

import jax
import jax.numpy as jnp
from jax import lax
from jax.experimental import pallas as pl
from jax.experimental.pallas import tpu as pltpu

F32 = jnp.float32
BF16 = jnp.bfloat16

LANES = 128
SUBLANES = 8
VMEM_LIMIT_BYTES = 56 * 1024 * 1024

D = 1024
CTX = 256
SEQ = 2048
NT = CTX + SEQ
DEPTH = 4
N_MOD = 6
MLP_HIDDEN = 4 * D
NORM_EPS = 1e-6
ROPE_THETA = 10000.0
GRID_W = 64

DA_HEAD_DIM = 64
DA_HEADS = 8
DA_SCALE = DA_HEAD_DIM ** -0.5

MLA_HEADS = 16
MLA_Q_LORA = 256
MLA_KV_LORA = 128
MLA_NOPE = 64
MLA_ROPE = 32
MLA_V = 64
MLA_SCALE = (MLA_NOPE + MLA_ROPE) ** -0.5
LOG2E = math.log2(math.e)

SC_WIDTH = 3

TM = CTX
NTILES = NT // TM
MLP_CHUNK = 1024
MOD_ROWS = 24
ATTN_PAIRS = 4


def _params(n_axes):
    return pltpu.CompilerParams(
        dimension_semantics=("arbitrary",) * n_axes,
        vmem_limit_bytes=VMEM_LIMIT_BYTES,
    )


def _rms(x, g):
    ms = jnp.mean(x * x, axis=-1, keepdims=True)
    return x * lax.rsqrt(ms + NORM_EPS) * g


def _modnorm(h, g, mod):
    return _rms(h, g) * (1.0 + mod[1:2, :]) + mod[0:1, :]


def _softmax_pv(s, v):
    e = jnp.exp2(s - jnp.max(s, axis=-1, keepdims=True)).astype(BF16)
    o = jnp.dot(e, jnp.concatenate([v, jnp.ones_like(v)], axis=1), preferred_element_type=F32)
    return o[:, :LANES] / o[:, LANES:]


def _rope_block(x, rope_ref, sh_hi, sh_lo):
    return (x * rope_ref[0]
            + pltpu.roll(x, sh_hi, 1) * rope_ref[1]
            + pltpu.roll(x, sh_lo, 1) * rope_ref[2])


def _ada_kernel(c_ref, w_ref, b_ref, o_ref):
    c = c_ref[...]
    s = c / (1.0 + jnp.exp(-c))
    o_ref[...] = jnp.dot(s.astype(BF16), w_ref[...].astype(BF16),
                         preferred_element_type=F32) + b_ref[...]


def _ada_all_layers(cc, w_ada, b_ada):
    return pl.pallas_call(
        _ada_kernel,
        grid=(DEPTH, N_MOD),
        in_specs=[
            pl.BlockSpec((MOD_ROWS, D), lambda l, n: (0, 0)),
            pl.BlockSpec((None, D, D), lambda l, n: (l, 0, n)),
            pl.BlockSpec((None, 1, D), lambda l, n: (l, 0, n)),
        ],
        out_specs=pl.BlockSpec((None, MOD_ROWS, D), lambda l, n: (l, 0, n)),
        out_shape=jax.ShapeDtypeStruct((DEPTH, MOD_ROWS, N_MOD * D), F32),
        compiler_params=_params(2),
        name="ada_mod",
    )(cc, w_ada, b_ada.reshape(DEPTH, 1, N_MOD * D))


def _mod_spec(layer, batch, first_tile):
    def index(b, t):
        row = jnp.where(t + first_tile == 0, batch, b)
        return (layer, row, 0, 0)
    return pl.BlockSpec((None, None, N_MOD, D), index)


def _tile_spec(width, first_tile=0):
    return pl.BlockSpec((None, TM, width), lambda b, t: (b, t + first_tile, 0))


def _full_spec(shape):
    nd = len(shape)
    return pl.BlockSpec(shape, lambda b, t: (0,) * nd)


def _da_pre_kernel(h_ref, mod_ref, g_ref, w_ref, rope_ref, q_ref, k_ref, v_ref):
    a = _modnorm(h_ref[...], g_ref[0:1, :], mod_ref[...])
    qkv = jnp.dot(a.astype(BF16), w_ref[...], preferred_element_type=F32)
    for j in range(D // LANES):
        lo, hi = j * LANES, (j + 1) * LANES
        q = _rope_block(qkv[:, lo:hi], rope_ref, LANES - 32, 32)
        q_ref[:, lo:hi] = (q * (DA_SCALE * LOG2E)).astype(BF16)
        k = _rope_block(qkv[:, D + lo:D + hi], rope_ref, LANES - 32, 32)
        k_ref[:, lo:hi] = k.astype(BF16)
    v_ref[...] = qkv[:, 2 * D:].astype(BF16)


def _da_pre(h, mods, layer, g, w_qkv, rope):
    b = h.shape[0]
    out = jax.ShapeDtypeStruct((b, NT, D), BF16)
    return pl.pallas_call(
        _da_pre_kernel,
        grid=(b, NTILES),
        in_specs=[
            _tile_spec(D),
            _mod_spec(layer, b, 0),
            _full_spec((4, D)),
            _full_spec((D, 3 * D)),
            pl.BlockSpec((3, TM, LANES), lambda b_, t: (0, t, 0)),
        ],
        out_specs=[_tile_spec(D)] * 3,
        out_shape=[out] * 3,
        compiler_params=_params(2),
        name="da_pre",
    )(h, mods, g, w_qkv, rope)


def _da_attn_kernel(lam_ref, sub_ref, q_ref, k_ref, v_ref, o_ref, *, lambda_init, first_tile):
    lv = lam_ref[...]
    lam = (jnp.exp(jnp.sum(lv[0:1] * lv[1:2], axis=-1, keepdims=True))
           - jnp.exp(jnp.sum(lv[2:3] * lv[3:4], axis=-1, keepdims=True))
           + lambda_init)

    def attend(nk):
        for p in range(ATTN_PAIRS):
            lo, hi = p * LANES, (p + 1) * LANES
            q = q_ref[:, lo:hi]
            k = k_ref[0:nk, lo:hi]
            v = v_ref[0:nk, lo:hi]
            lane = lax.broadcasted_iota(jnp.int32, q.shape, 1)
            zero = jnp.zeros_like(q)
            pv = [_softmax_pv(lax.dot_general(qm, k, (((1,), (1,)), ((), ())),
                                              preferred_element_type=F32), v)
                  for qm in (jnp.where(lane < DA_HEAD_DIM, q, zero),
                             jnp.where(lane >= DA_HEAD_DIM, q, zero))]
            o = pv[0] - lam * pv[1]
            o_ref[:, lo:hi] = (_rms(o, sub_ref[...]) * (1.0 - lambda_init)).astype(BF16)

    if first_tile == 0:
        t = pl.program_id(2)
        pl.when(t == 0)(lambda: attend(CTX))
        pl.when(t > 0)(lambda: attend(NT))
    else:
        attend(NT)


def _da_attn(q, k, v, lam_vecs, subln, lambda_init, ctx_out):
    b = q.shape[0]
    first_tile = 0 if ctx_out else 1
    width = ATTN_PAIRS * LANES
    kv_spec = pl.BlockSpec((None, NT, width), lambda b_, h, t: (b_, 0, h))
    qo_spec = pl.BlockSpec((None, TM, width), lambda b_, h, t: (b_, t + first_tile, h))
    return pl.pallas_call(
        functools.partial(_da_attn_kernel, lambda_init=lambda_init, first_tile=first_tile),
        grid=(b, DA_HEADS // ATTN_PAIRS, NTILES - first_tile),
        in_specs=[
            pl.BlockSpec((4, DA_HEAD_DIM), lambda b_, h, t: (0, 0)),
            pl.BlockSpec((1, LANES), lambda b_, h, t: (0, 0)),
            qo_spec, kv_spec, kv_spec,
        ],
        out_specs=qo_spec,
        out_shape=jax.ShapeDtypeStruct((b, NT, D), BF16),
        compiler_params=_params(3),
        name="da_attn",
    )(lam_vecs, subln.reshape(1, LANES), q, k, v)


MLA_DOWN_W = 512
MLA_Q_W = MLA_HEADS * LANES
MLA_KV_W = MLA_HEADS * LANES + MLA_HEADS * MLA_V


def _mla_pre_kernel(h_ref, mod_ref, g_ref, wd_ref, qg_ref, wq_ref, kg_ref, wkv_ref, rope_ref,
                    q_ref, k_ref, v_ref):
    a = _modnorm(h_ref[...], g_ref[0:1, :], mod_ref[...])
    d = jnp.dot(a.astype(BF16), wd_ref[...], preferred_element_type=F32)
    cq = _rms(d[:, :MLA_Q_LORA], qg_ref[...])
    ckv = _rms(d[:, MLA_Q_LORA:MLA_Q_LORA + MLA_KV_LORA], kg_ref[...])
    kr = _rope_block(d[:, MLA_Q_LORA + MLA_KV_LORA:], rope_ref, LANES - 16, 16)
    q = jnp.dot(cq.astype(BF16), wq_ref[...], preferred_element_type=F32)
    kv = jnp.dot(ckv.astype(BF16), wkv_ref[...], preferred_element_type=F32)
    for j in range(MLA_HEADS):
        lo, hi = j * LANES, (j + 1) * LANES
        qj = _rope_block(q[:, lo:hi], rope_ref, LANES - 16, 16)
        q_ref[:, lo:hi] = (qj * (MLA_SCALE * LOG2E)).astype(BF16)
        k_ref[:, lo:hi] = (kv[:, lo:hi] + kr).astype(BF16)
    v_ref[...] = kv[:, MLA_HEADS * LANES:].astype(BF16)


def _mla_pre(h, mods, layer, g, wd, qg, wq, kg, wkv, rope):
    b = h.shape[0]
    return pl.pallas_call(
        _mla_pre_kernel,
        grid=(b, NTILES),
        in_specs=[
            _tile_spec(D),
            _mod_spec(layer, b, 0),
            _full_spec((4, D)),
            _full_spec((D, MLA_DOWN_W)),
            _full_spec((1, MLA_Q_LORA)),
            _full_spec((MLA_Q_LORA, MLA_Q_W)),
            _full_spec((1, MLA_KV_LORA)),
            _full_spec((MLA_KV_LORA, MLA_KV_W)),
            pl.BlockSpec((3, TM, LANES), lambda b_, t: (0, t, 0)),
        ],
        out_specs=[_tile_spec(MLA_Q_W), _tile_spec(MLA_Q_W), _tile_spec(D)],
        out_shape=[jax.ShapeDtypeStruct((b, NT, MLA_Q_W), BF16),
                   jax.ShapeDtypeStruct((b, NT, MLA_Q_W), BF16),
                   jax.ShapeDtypeStruct((b, NT, D), BF16)],
        compiler_params=_params(2),
        name="mla_pre",
    )(h, mods, g, wd, qg, wq, kg, wkv, rope)


def _mla_attn_kernel(q_ref, k_ref, v_ref, o_ref, *, first_tile):
    def attend(nk):
        for p in range(ATTN_PAIRS):
            v = v_ref[0:nk, p * LANES:(p + 1) * LANES]
            pv = [_softmax_pv(lax.dot_general(q_ref[:, h * LANES:(h + 1) * LANES],
                                              k_ref[0:nk, h * LANES:(h + 1) * LANES],
                                              (((1,), (1,)), ((), ())), preferred_element_type=F32), v)
                  for h in (2 * p, 2 * p + 1)]
            lane = lax.broadcasted_iota(jnp.int32, (TM, LANES), 1)
            o_ref[:, p * LANES:(p + 1) * LANES] = jnp.where(lane < MLA_V, pv[0], pv[1]).astype(BF16)

    if first_tile == 0:
        t = pl.program_id(2)
        pl.when(t == 0)(lambda: attend(CTX))
        pl.when(t > 0)(lambda: attend(NT))
    else:
        attend(NT)


def _mla_attn(q, k, v, ctx_out):
    b = q.shape[0]
    first_tile = 0 if ctx_out else 1
    steps = MLA_HEADS // (2 * ATTN_PAIRS)
    width = ATTN_PAIRS * LANES
    return pl.pallas_call(
        functools.partial(_mla_attn_kernel, first_tile=first_tile),
        grid=(b, steps, NTILES - first_tile),
        in_specs=[
            pl.BlockSpec((None, TM, 2 * width), lambda b_, h, t: (b_, t + first_tile, h)),
            pl.BlockSpec((None, NT, 2 * width), lambda b_, h, t: (b_, 0, h)),
            pl.BlockSpec((None, NT, width), lambda b_, h, t: (b_, 0, h)),
        ],
        out_specs=pl.BlockSpec((None, TM, width), lambda b_, h, t: (b_, t + first_tile, h)),
        out_shape=jax.ShapeDtypeStruct((b, NT, D), BF16),
        compiler_params=_params(3),
        name="mla_attn",
    )(q, k, v)


def _sc_pre_kernel(hp_ref, h_ref, hn_ref, mod_ref, g_ref, w_ref, cw_ref, y_ref):
    t = pl.program_id(1)
    rows = jnp.concatenate([hp_ref[...], h_ref[...], hn_ref[...]], axis=0)
    a = _modnorm(rows, g_ref[0:1, :], mod_ref[...])
    p = jnp.dot(a.astype(BF16), w_ref[...], preferred_element_type=F32)
    z = p[:, D:2 * D] * p[:, 2 * D:]
    b_gate = p[SUBLANES:SUBLANES + TM, :D]
    row = lax.broadcasted_iota(jnp.int32, (TM, 1), 0)
    first = jnp.logical_and(row == 0, t <= 1)
    last = jnp.logical_and(row == TM - 1, jnp.logical_or(t == 0, t == NTILES - 1))
    z_prev = jnp.where(first, 0.0, z[SUBLANES - 1:SUBLANES - 1 + TM])
    z_next = jnp.where(last, 0.0, z[SUBLANES + 1:SUBLANES + 1 + TM])
    u = cw_ref[0:1, :] * z_prev + cw_ref[1:2, :] * z[SUBLANES:SUBLANES + TM] + cw_ref[2:3, :] * z_next
    y_ref[...] = (b_gate * u).astype(BF16)


def _sc_pre(h, mods, layer, g, w_in, conv_w):
    b = h.shape[0]
    per_tile = TM // SUBLANES
    last_halo = NT // SUBLANES - 1
    return pl.pallas_call(
        _sc_pre_kernel,
        grid=(b, NTILES),
        in_specs=[
            pl.BlockSpec((None, SUBLANES, D),
                         lambda b_, t: (b_, jnp.maximum(t * per_tile - 1, 0), 0)),
            _tile_spec(D),
            pl.BlockSpec((None, SUBLANES, D),
                         lambda b_, t: (b_, jnp.minimum((t + 1) * per_tile, last_halo), 0)),
            _mod_spec(layer, b, 0),
            _full_spec((4, D)),
            _full_spec((D, 3 * D)),
            _full_spec((SC_WIDTH, D)),
        ],
        out_specs=_tile_spec(D),
        out_shape=jax.ShapeDtypeStruct((b, NT, D), BF16),
        compiler_params=_params(2),
        name="sc_pre",
    )(h, h, h, mods, g, w_in, conv_w)


def _post_kernel(h_ref, y_ref, mod_ref, g_ref, wo_ref, w1_ref, w2_ref, o_ref):
    mod = mod_ref[...]
    y = jnp.dot(y_ref[...], wo_ref[...], preferred_element_type=F32)
    h = h_ref[...] + mod[2:3, :] * _rms(y, g_ref[1:2, :])
    a = (_rms(h, g_ref[2:3, :]) * (1.0 + mod[4:5, :]) + mod[3:4, :]).astype(BF16)
    f = jnp.zeros((TM, D), F32)
    for c in range(MLP_HIDDEN // MLP_CHUNK):
        lo, hi = c * MLP_CHUNK, (c + 1) * MLP_CHUNK
        u = jnp.maximum(jnp.dot(a, w1_ref[:, lo:hi], preferred_element_type=F32), 0.0)
        f = f + jnp.dot((u * u).astype(BF16), w2_ref[lo:hi, :], preferred_element_type=F32)
    o_ref[...] = h + mod[5:6, :] * _rms(f, g_ref[3:4, :])


def _post(h, y, mods, layer, g, w_out, w1, w2, ctx_out):
    b = h.shape[0]
    first_tile = 0 if ctx_out else 1
    n_out = NT if ctx_out else SEQ
    return pl.pallas_call(
        _post_kernel,
        grid=(b, NTILES - first_tile),
        in_specs=[
            _tile_spec(D, first_tile),
            _tile_spec(D, first_tile),
            _mod_spec(layer, b, first_tile),
            _full_spec((4, D)),
            _full_spec((D, D)),
            _full_spec((D, MLP_HIDDEN)),
            _full_spec((MLP_HIDDEN, D)),
        ],
        out_specs=_tile_spec(D),
        out_shape=jax.ShapeDtypeStruct((b, n_out, D), F32),
        compiler_params=_params(2),
        name="post_mlp",
    )(h, y, mods, g, w_out, w1, w2)


def _rope_tables(rot_dim, lane0, chunk):
    quarter = rot_dim // 4
    half = rot_dim // 2
    pos = jnp.arange(SEQ, dtype=F32)
    row = jnp.floor(pos / GRID_W)
    col = pos - row * GRID_W
    inv_freq = ROPE_THETA ** (-jnp.arange(quarter, dtype=F32) / quarter)
    ang = jnp.concatenate([row[:, None] * inv_freq, col[:, None] * inv_freq], axis=-1)
    cos, sin = jnp.cos(ang), jnp.sin(ang)
    lane = jnp.arange(LANES)
    off = (lane % chunk) - lane0
    in_rot = (off >= 0) & (off < rot_dim)
    first = in_rot & (off < half)
    second = in_rot & (off >= half)
    idx = jnp.clip(off % half, 0, half - 1)
    c_tab = jnp.where(in_rot[None, :], cos[:, idx], 1.0)
    s_hi = jnp.where(first[None, :], -sin[:, idx], 0.0)
    s_lo = jnp.where(second[None, :], sin[:, idx], 0.0)
    lat = jnp.stack([c_tab, s_hi, s_lo])
    ctx = jnp.stack([jnp.ones((CTX, LANES), F32), jnp.zeros((CTX, LANES), F32),
                     jnp.zeros((CTX, LANES), F32)])
    return jnp.concatenate([ctx, lat], axis=1)


def _mla_weight_layouts(w_down, w_uq, w_ukv):
    zeros = functools.partial(jnp.zeros, dtype=w_down.dtype)
    n_lat = MLA_Q_LORA + MLA_KV_LORA
    wd = jnp.concatenate([w_down[:, :n_lat], zeros((D, MLA_NOPE)), w_down[:, n_lat:],
                          zeros((D, LANES - MLA_NOPE - MLA_ROPE))], axis=1)
    wq = w_uq.reshape(MLA_Q_LORA, MLA_HEADS, MLA_NOPE + MLA_ROPE)
    wq = jnp.pad(wq, ((0, 0), (0, 0), (0, LANES - MLA_NOPE - MLA_ROPE))).reshape(MLA_Q_LORA, MLA_Q_W)
    wkv = w_ukv.reshape(MLA_KV_LORA, MLA_HEADS, MLA_NOPE + MLA_V)
    wk = jnp.pad(wkv[..., :MLA_NOPE], ((0, 0), (0, 0), (0, LANES - MLA_NOPE)))
    wkv = jnp.concatenate([wk.reshape(MLA_KV_LORA, MLA_HEADS * LANES),
                           wkv[..., MLA_NOPE:].reshape(MLA_KV_LORA, MLA_HEADS * MLA_V)], axis=1)
    return wd.astype(BF16), wq.astype(BF16), wkv.astype(BF16)


def kernel(x, c, ctx, c_ctx, w_ada, b_ada, norm_g, w_mlp_in, w_mlp_out, w_da_qkv, da_lambda, da_subln, w_da_out, w_mla_down, mla_q_norm, w_mla_uq, mla_kv_norm, w_mla_ukv, w_mla_out, w_sc_in, sc_conv, w_sc_out):
    b = x.shape[0]
    assert x.shape == (b, SEQ, D) and ctx.shape == (b, CTX, D) and b + 1 <= MOD_ROWS

    cc = jnp.concatenate([c, c_ctx[None, :], jnp.zeros((MOD_ROWS - b - 1, D), F32)], axis=0)
    mods = _ada_all_layers(cc, w_ada, b_ada).reshape(DEPTH, MOD_ROWS, N_MOD, D)

    rope_da = _rope_tables(DA_HEAD_DIM, 0, DA_HEAD_DIM)
    rope_mla = _rope_tables(MLA_ROPE, MLA_NOPE, LANES)

    h = jnp.concatenate([ctx, x], axis=1)
    for i in range(DEPTH):
        kind, j = i % 3, i // 3
        ctx_out = i < DEPTH - 1
        g = norm_g[i]
        if kind == 0:
            lambda_init = 0.8 - 0.6 * math.exp(-0.3 * i)
            q, k, v = _da_pre(h, mods, i, g, w_da_qkv[j].astype(BF16), rope_da)
            y = _da_attn(q, k, v, da_lambda[j], da_subln[j], lambda_init, ctx_out)
            w_out = w_da_out[j]
        elif kind == 1:
            wd, wq, wkv = _mla_weight_layouts(w_mla_down[j], w_mla_uq[j], w_mla_ukv[j])
            q, k, v = _mla_pre(h, mods, i, g, wd, mla_q_norm[j].reshape(1, -1), wq,
                               mla_kv_norm[j].reshape(1, -1), wkv, rope_mla)
            y = _mla_attn(q, k, v, ctx_out)
            w_out = w_mla_out[j]
        else:
            y = _sc_pre(h, mods, i, g, w_sc_in[j].astype(BF16), sc_conv[j])
            w_out = w_sc_out[j]
        h = _post(h, y, mods, i, g, w_out.astype(BF16), w_mlp_in[i].astype(BF16),
                  w_mlp_out[i].astype(BF16), ctx_out)
    return h
```

```python
import functools
import math

import jax
import jax.numpy as jnp
from jax import lax
from jax.experimental import pallas as pl
from jax.experimental.pallas import tpu as pltpu

F32 = jnp.float32
BF16 = jnp.bfloat16

LANES = 128
SUBLANES = 8
VMEM_LIMIT_BYTES = 56 * 1024 * 1024

D = 1024
CTX = 256
SEQ = 2048
NT = CTX + SEQ
DEPTH = 4
N_MOD = 6
MLP_HIDDEN = 4 * D
NORM_EPS = 1e-6
ROPE_THETA = 10000.0
GRID_W = 64

DA_HEAD_DIM = 64
DA_HEADS = 8
DA_SCALE = DA_HEAD_DIM ** -0.5

MLA_HEADS = 16
MLA_Q_LORA = 256
MLA_KV_LORA = 128
MLA_NOPE = 64
MLA_ROPE = 32
MLA_V = 64
MLA_SCALE = (MLA_NOPE + MLA_ROPE) ** -0.5
LOG2E = math.log2(math.e)

SC_WIDTH = 3

TM = CTX
NTILES = NT // TM
MLP_CHUNK = 1024
MOD_ROWS = 24
ATTN_PAIRS = 4


def _params(n_axes):
    return pltpu.CompilerParams(
        dimension_semantics=("arbitrary",) * n_axes,
        vmem_limit_bytes=VMEM_LIMIT_BYTES,
    )


def _rms(x, g):
    ms = jnp.mean(x * x, axis=-1, keepdims=True)
    return x * lax.rsqrt(ms + NORM_EPS) * g


def _modnorm(h, g, mod):
    return _rms(h, g) * (1.0 + mod[1:2, :]) + mod[0:1, :]


def _softmax_pv(scores, v):
    v_ones = jnp.concatenate([v, jnp.ones_like(v)], axis=1)
    es = [jnp.exp2(s - jnp.max(s, axis=-1, keepdims=True)).astype(BF16) for s in scores]
    os_ = [jnp.dot(e, v_ones, preferred_element_type=F32) for e in es]
    return [o[:, :LANES] / o[:, LANES:] for o in os_]


def _pipelined(n, first_stage, second_stage):
    carried = first_stage(0)
    for i in range(1, n):
        upcoming = first_stage(i)
        second_stage(i - 1, carried)
        carried = upcoming
    second_stage(n - 1, carried)


def _rope_block(x, rope_ref, sh_hi, sh_lo):
    return (x * rope_ref[0]
            + pltpu.roll(x, sh_hi, 1) * rope_ref[1]
            + pltpu.roll(x, sh_lo, 1) * rope_ref[2])


def _ada_kernel(c_ref, w_ref, b_ref, o_ref):
    c = c_ref[...]
    s = c / (1.0 + jnp.exp(-c))
    o_ref[...] = jnp.dot(s.astype(BF16), w_ref[...].astype(BF16),
                         preferred_element_type=F32) + b_ref[...]


def _ada_all_layers(cc, w_ada, b_ada):
    return pl.pallas_call(
        _ada_kernel,
        grid=(DEPTH, N_MOD),
        in_specs=[
            pl.BlockSpec((MOD_ROWS, D), lambda l, n: (0, 0)),
            pl.BlockSpec((None, D, D), lambda l, n: (l, 0, n)),
            pl.BlockSpec((None, 1, D), lambda l, n: (l, 0, n)),
        ],
        out_specs=pl.BlockSpec((None, MOD_ROWS, D), lambda l, n: (l, 0, n)),
        out_shape=jax.ShapeDtypeStruct((DEPTH, MOD_ROWS, N_MOD * D), F32),
        compiler_params=_params(2),
        name="ada_mod",
    )(cc, w_ada, b_ada.reshape(DEPTH, 1, N_MOD * D))


def _mod_spec(layer, batch, first_tile):
    def index(b, t):
        row = jnp.where(t + first_tile == 0, batch, b)
        return (layer, row, 0, 0)
    return pl.BlockSpec((None, None, N_MOD, D), index)


def _tile_spec(width, first_tile=0):
    return pl.BlockSpec((None, TM, width), lambda b, t: (b, t + first_tile, 0))


def _full_spec(shape):
    nd = len(shape)
    return pl.BlockSpec(shape, lambda b, t: (0,) * nd, pipeline_mode=pl.Buffered(1))


def _da_pre_kernel(h_ref, mod_ref, g_ref, w_ref, rope_ref, q_ref, k_ref, v_ref):
    a = _modnorm(h_ref[...], g_ref[0:1, :], mod_ref[...])
    qkv = jnp.dot(a.astype(BF16), w_ref[...], preferred_element_type=F32)
    for j in range(D // LANES):
        lo, hi = j * LANES, (j + 1) * LANES
        q = _rope_block(qkv[:, lo:hi], rope_ref, LANES - 32, 32)
        q_ref[:, lo:hi] = (q * (DA_SCALE * LOG2E)).astype(BF16)
        k = _rope_block(qkv[:, D + lo:D + hi], rope_ref, LANES - 32, 32)
        k_ref[:, lo:hi] = k.astype(BF16)
    v_ref[...] = qkv[:, 2 * D:].astype(BF16)


def _da_pre(h, mods, layer, g, w_qkv, rope):
    b = h.shape[0]
    out = jax.ShapeDtypeStruct((b, NT, D), BF16)
    return pl.pallas_call(
        _da_pre_kernel,
        grid=(b, NTILES),
        in_specs=[
            _tile_spec(D),
            _mod_spec(layer, b, 0),
            _full_spec((4, D)),
            _full_spec((D, 3 * D)),
            pl.BlockSpec((3, TM, LANES), lambda b_, t: (0, t, 0)),
        ],
        out_specs=[_tile_spec(D)] * 3,
        out_shape=[out] * 3,
        compiler_params=_params(2),
        name="da_pre",
    )(h, mods, g, w_qkv, rope)


def _da_attn_kernel(lam_ref, sub_ref, q_ref, k_ref, v_ref, o_ref, *, lambda_init, first_tile):
    lv = lam_ref[...]
    lam = (jnp.exp(jnp.sum(lv[0:1] * lv[1:2], axis=-1, keepdims=True))
           - jnp.exp(jnp.sum(lv[2:3] * lv[3:4], axis=-1, keepdims=True))
           + lambda_init)

    def attend(nk):
        def scores(p):
            lo, hi = p * LANES, (p + 1) * LANES
            q = q_ref[:, lo:hi]
            lane = lax.broadcasted_iota(jnp.int32, q.shape, 1)
            zero = jnp.zeros_like(q)
            return [lax.dot_general(qm, k_ref[0:nk, lo:hi], (((1,), (1,)), ((), ())),
                                    preferred_element_type=F32)
                    for qm in (jnp.where(lane < DA_HEAD_DIM, q, zero),
                               jnp.where(lane >= DA_HEAD_DIM, q, zero))]

        def finish(p, s):
            lo, hi = p * LANES, (p + 1) * LANES
            pv = _softmax_pv(s, v_ref[0:nk, lo:hi])
            o = pv[0] - lam * pv[1]
            o_ref[:, lo:hi] = (_rms(o, sub_ref[...]) * (1.0 - lambda_init)).astype(BF16)

        _pipelined(ATTN_PAIRS, scores, finish)

    if first_tile == 0:
        t = pl.program_id(2)
        pl.when(t == 0)(lambda: attend(CTX))
        pl.when(t > 0)(lambda: attend(NT))
    else:
        attend(NT)


def _da_attn(q, k, v, lam_vecs, subln, lambda_init, ctx_out):
    b = q.shape[0]
    first_tile = 0 if ctx_out else 1
    width = ATTN_PAIRS * LANES
    kv_spec = pl.BlockSpec((None, NT, width), lambda b_, h, t: (b_, 0, h))
    qo_spec = pl.BlockSpec((None, TM, width), lambda b_, h, t: (b_, t + first_tile, h))
    return pl.pallas_call(
        functools.partial(_da_attn_kernel, lambda_init=lambda_init, first_tile=first_tile),
        grid=(b, DA_HEADS // ATTN_PAIRS, NTILES - first_tile),
        in_specs=[
            pl.BlockSpec((4, DA_HEAD_DIM), lambda b_, h, t: (0, 0)),
            pl.BlockSpec((1, LANES), lambda b_, h, t: (0, 0)),
            qo_spec, kv_spec, kv_spec,
        ],
        out_specs=qo_spec,
        out_shape=jax.ShapeDtypeStruct((b, NT, D), BF16),
        compiler_params=_params(3),
        name="da_attn",
    )(lam_vecs, subln.reshape(1, LANES), q, k, v)


MLA_DOWN_W = 512
MLA_Q_W = MLA_HEADS * LANES
MLA_KV_W = MLA_HEADS * LANES + MLA_HEADS * MLA_V


def _mla_pre_kernel(h_ref, mod_ref, g_ref, wd_ref, qg_ref, wq_ref, kg_ref, wkv_ref, rope_ref,
                    q_ref, k_ref, v_ref):
    a = _modnorm(h_ref[...], g_ref[0:1, :], mod_ref[...])
    d = jnp.dot(a.astype(BF16), wd_ref[...], preferred_element_type=F32)
    cq = _rms(d[:, :MLA_Q_LORA], qg_ref[...])
    ckv = _rms(d[:, MLA_Q_LORA:MLA_Q_LORA + MLA_KV_LORA], kg_ref[...])
    kr = _rope_block(d[:, MLA_Q_LORA + MLA_KV_LORA:], rope_ref, LANES - 16, 16)
    q = jnp.dot(cq.astype(BF16), wq_ref[...], preferred_element_type=F32)
    kv = jnp.dot(ckv.astype(BF16), wkv_ref[...], preferred_element_type=F32)
    for j in range(MLA_HEADS):
        lo, hi = j * LANES, (j + 1) * LANES
        qj = _rope_block(q[:, lo:hi], rope_ref, LANES - 16, 16)
        q_ref[:, lo:hi] = (qj * (MLA_SCALE * LOG2E)).astype(BF16)
        k_ref[:, lo:hi] = (kv[:, lo:hi] + kr).astype(BF16)
    v_ref[...] = kv[:, MLA_HEADS * LANES:].astype(BF16)


def _mla_pre(h, mods, layer, g, wd, qg, wq, kg, wkv, rope):
    b = h.shape[0]
    return pl.pallas_call(
        _mla_pre_kernel,
        grid=(b, NTILES),
        in_specs=[
            _tile_spec(D),
            _mod_spec(layer, b, 0),
            _full_spec((4, D)),
            _full_spec((D, MLA_DOWN_W)),
            _full_spec((1, MLA_Q_LORA)),
            _full_spec((MLA_Q_LORA, MLA_Q_W)),
            _full_spec((1, MLA_KV_LORA)),
            _full_spec((MLA_KV_LORA, MLA_KV_W)),
            pl.BlockSpec((3, TM, LANES), lambda b_, t: (0, t, 0)),
        ],
        out_specs=[_tile_spec(MLA_Q_W), _tile_spec(MLA_Q_W), _tile_spec(D)],
        out_shape=[jax.ShapeDtypeStruct((b, NT, MLA_Q_W), BF16),
                   jax.ShapeDtypeStruct((b, NT, MLA_Q_W), BF16),
                   jax.ShapeDtypeStruct((b, NT, D), BF16)],
        compiler_params=_params(2),
        name="mla_pre",
    )(h, mods, g, wd, qg, wq, kg, wkv, rope)


def _mla_attn_kernel(q_ref, k_ref, v_ref, o_ref, *, first_tile):
    def attend(nk):
        def scores(p):
            return [lax.dot_general(q_ref[:, h * LANES:(h + 1) * LANES],
                                    k_ref[0:nk, h * LANES:(h + 1) * LANES],
                                    (((1,), (1,)), ((), ())), preferred_element_type=F32)
                    for h in (2 * p, 2 * p + 1)]

        def finish(p, s):
            pv = _softmax_pv(s, v_ref[0:nk, p * LANES:(p + 1) * LANES])
            lane = lax.broadcasted_iota(jnp.int32, (TM, LANES), 1)
            o_ref[:, p * LANES:(p + 1) * LANES] = jnp.where(lane < MLA_V, pv[0], pv[1]).astype(BF16)

        _pipelined(ATTN_PAIRS, scores, finish)

    if first_tile == 0:
        t = pl.program_id(2)
        pl.when(t == 0)(lambda: attend(CTX))
        pl.when(t > 0)(lambda: attend(NT))
    else:
        attend(NT)


def _mla_attn(q, k, v, ctx_out):
    b = q.shape[0]
    first_tile = 0 if ctx_out else 1
    steps = MLA_HEADS // (2 * ATTN_PAIRS)
    width = ATTN_PAIRS * LANES
    return pl.pallas_call(
        functools.partial(_mla_attn_kernel, first_tile=first_tile),
        grid=(b, steps, NTILES - first_tile),
        in_specs=[
            pl.BlockSpec((None, TM, 2 * width), lambda b_, h, t: (b_, t + first_tile, h)),
            pl.BlockSpec((None, NT, 2 * width), lambda b_, h, t: (b_, 0, h)),
            pl.BlockSpec((None, NT, width), lambda b_, h, t: (b_, 0, h)),
        ],
        out_specs=pl.BlockSpec((None, TM, width), lambda b_, h, t: (b_, t + first_tile, h)),
        out_shape=jax.ShapeDtypeStruct((b, NT, D), BF16),
        compiler_params=_params(3),
        name="mla_attn",
    )(q, k, v)


def _sc_pre_kernel(hp_ref, h_ref, hn_ref, mod_ref, g_ref, w_ref, cw_ref, y_ref):
    t = pl.program_id(1)
    rows = jnp.concatenate([hp_ref[...], h_ref[...], hn_ref[...]], axis=0)
    a = _modnorm(rows, g_ref[0:1, :], mod_ref[...])
    p = jnp.dot(a.astype(BF16), w_ref[...], preferred_element_type=F32)
    z = p[:, D:2 * D] * p[:, 2 * D:]
    b_gate = p[SUBLANES:SUBLANES + TM, :D]
    row = lax.broadcasted_iota(jnp.int32, (TM, 1), 0)
    first = jnp.logical_and(row == 0, t <= 1)
    last = jnp.logical_and(row == TM - 1, jnp.logical_or(t == 0, t == NTILES - 1))
    z_prev = jnp.where(first, 0.0, z[SUBLANES - 1:SUBLANES - 1 + TM])
    z_next = jnp.where(last, 0.0, z[SUBLANES + 1:SUBLANES + 1 + TM])
    u = cw_ref[0:1, :] * z_prev + cw_ref[1:2, :] * z[SUBLANES:SUBLANES + TM] + cw_ref[2:3, :] * z_next
    y_ref[...] = (b_gate * u).astype(BF16)


def _sc_pre(h, mods, layer, g, w_in, conv_w):
    b = h.shape[0]
    per_tile = TM // SUBLANES
    last_halo = NT // SUBLANES - 1
    return pl.pallas_call(
        _sc_pre_kernel,
        grid=(b, NTILES),
        in_specs=[
            pl.BlockSpec((None, SUBLANES, D),
                         lambda b_, t: (b_, jnp.maximum(t * per_tile - 1, 0), 0)),
            _tile_spec(D),
            pl.BlockSpec((None, SUBLANES, D),
                         lambda b_, t: (b_, jnp.minimum((t + 1) * per_tile, last_halo), 0)),
            _mod_spec(layer, b, 0),
            _full_spec((4, D)),
            _full_spec((D, 3 * D)),
            _full_spec((SC_WIDTH, D)),
        ],
        out_specs=_tile_spec(D),
        out_shape=jax.ShapeDtypeStruct((b, NT, D), BF16),
        compiler_params=_params(2),
        name="sc_pre",
    )(h, h, h, mods, g, w_in, conv_w)


def _post_kernel(*refs, tiles):
    h_refs, y_refs, mod_refs = refs[:tiles], refs[tiles:2 * tiles], refs[2 * tiles:3 * tiles]
    g_ref, wo_ref, w1_ref, w2_ref, o_ref = refs[3 * tiles:]
    r = range(tiles)
    mod = [mod_refs[c][...] for c in r]
    y = [jnp.dot(y_refs[c][...], wo_ref[...], preferred_element_type=F32) for c in r]
    h = [h_refs[c][...] + mod[c][2:3, :] * _rms(y[c], g_ref[1:2, :]) for c in r]
    a = [(_rms(h[c], g_ref[2:3, :]) * (1.0 + mod[c][4:5, :]) + mod[c][3:4, :]).astype(BF16) for c in r]
    f = [jnp.zeros((TM, D), F32) for c in r]
    for m in range(MLP_HIDDEN // MLP_CHUNK):
        lo, hi = m * MLP_CHUNK, (m + 1) * MLP_CHUNK
        u = [jnp.maximum(jnp.dot(a[c], w1_ref[:, lo:hi], preferred_element_type=F32), 0.0) for c in r]
        f = [f[c] + jnp.dot((u[c] * u[c]).astype(BF16), w2_ref[lo:hi, :], preferred_element_type=F32)
             for c in r]
    for c in r:
        o_ref[c * TM:(c + 1) * TM, :] = h[c] + mod[c][5:6, :] * _rms(f[c], g_ref[3:4, :])


def _post(h, y, mods, layer, g, w_out, w1, w2, ctx_out):
    b = h.shape[0]
    first_tile = 0 if ctx_out else 1
    n_tiles = NTILES - first_tile
    tiles = 3 if n_tiles % 3 == 0 else 2
    assert n_tiles % tiles == 0

    def tile_spec(c):
        return pl.BlockSpec((None, TM, D), lambda b_, j: (b_, tiles * j + c + first_tile, 0))

    def mod_spec(c):
        def index(b_, j):
            return (layer, jnp.where(tiles * j + c + first_tile == 0, b, b_), 0, 0)
        return pl.BlockSpec((None, None, N_MOD, D), index)

    chain = range(tiles)
    return pl.pallas_call(
        functools.partial(_post_kernel, tiles=tiles),
        grid=(b, n_tiles // tiles),
        in_specs=[tile_spec(c) for c in chain] + [tile_spec(c) for c in chain]
                 + [mod_spec(c) for c in chain]
                 + [_full_spec((4, D)), _full_spec((D, D)), _full_spec((D, MLP_HIDDEN)),
                    _full_spec((MLP_HIDDEN, D))],
        out_specs=pl.BlockSpec((None, tiles * TM, D), lambda b_, j: (b_, j, 0)),
        out_shape=jax.ShapeDtypeStruct((b, n_tiles * TM, D), F32),
        compiler_params=_params(2),
        name="post_mlp",
    )(*([h] * tiles), *([y] * tiles), *([mods] * tiles), g, w_out, w1, w2)


def _rope_tables(rot_dim, lane0, chunk):
    quarter = rot_dim // 4
    half = rot_dim // 2
    pos = jnp.arange(SEQ, dtype=F32)
    row = jnp.floor(pos / GRID_W)
    col = pos - row * GRID_W
    inv_freq = ROPE_THETA ** (-jnp.arange(quarter, dtype=F32) / quarter)
    ang = jnp.concatenate([row[:, None] * inv_freq, col[:, None] * inv_freq], axis=-1)
    cos, sin = jnp.cos(ang), jnp.sin(ang)
    lane = jnp.arange(LANES)
    off = (lane % chunk) - lane0
    in_rot = (off >= 0) & (off < rot_dim)
    first = in_rot & (off < half)
    second = in_rot & (off >= half)
    idx = jnp.clip(off % half, 0, half - 1)
    c_tab = jnp.where(in_rot[None, :], cos[:, idx], 1.0)
    s_hi = jnp.where(first[None, :], -sin[:, idx], 0.0)
    s_lo = jnp.where(second[None, :], sin[:, idx], 0.0)
    lat = jnp.stack([c_tab, s_hi, s_lo])
    ctx = jnp.stack([jnp.ones((CTX, LANES), F32), jnp.zeros((CTX, LANES), F32),
                     jnp.zeros((CTX, LANES), F32)])
    return jnp.concatenate([ctx, lat], axis=1)


def _mla_weight_layouts(w_down, w_uq, w_ukv):
    zeros = functools.partial(jnp.zeros, dtype=w_down.dtype)
    n_lat = MLA_Q_LORA + MLA_KV_LORA
    wd = jnp.concatenate([w_down[:, :n_lat], zeros((D, MLA_NOPE)), w_down[:, n_lat:],
                          zeros((D, LANES - MLA_NOPE - MLA_ROPE))], axis=1)
    wq = w_uq.reshape(MLA_Q_LORA, MLA_HEADS, MLA_NOPE + MLA_ROPE)
    wq = jnp.pad(wq, ((0, 0), (0, 0), (0, LANES - MLA_NOPE - MLA_ROPE))).reshape(MLA_Q_LORA, MLA_Q_W)
    wkv = w_ukv.reshape(MLA_KV_LORA, MLA_HEADS, MLA_NOPE + MLA_V)
    wk = jnp.pad(wkv[..., :MLA_NOPE], ((0, 0), (0, 0), (0, LANES - MLA_NOPE)))
    wkv = jnp.concatenate([wk.reshape(MLA_KV_LORA, MLA_HEADS * LANES),
                           wkv[..., MLA_NOPE:].reshape(MLA_KV_LORA, MLA_HEADS * MLA_V)], axis=1)
    return wd.astype(BF16), wq.astype(BF16), wkv.astype(BF16)


def kernel(x, c, ctx, c_ctx, w_ada, b_ada, norm_g, w_mlp_in, w_mlp_out, w_da_qkv, da_lambda, da_subln, w_da_out, w_mla_down, mla_q_norm, w_mla_uq, mla_kv_norm, w_mla_ukv, w_mla_out, w_sc_in, sc_conv, w_sc_out):
    b = x.shape[0]
    assert x.shape == (b, SEQ, D) and ctx.shape == (b, CTX, D) and b + 1 <= MOD_ROWS

    cc = jnp.concatenate([c, c_ctx[None, :], jnp.zeros((MOD_ROWS - b - 1, D), F32)], axis=0)
    mods = _ada_all_layers(cc, w_ada, b_ada).reshape(DEPTH, MOD_ROWS, N_MOD, D)

    rope_da = _rope_tables(DA_HEAD_DIM, 0, DA_HEAD_DIM)
    rope_mla = _rope_tables(MLA_ROPE, MLA_NOPE, LANES)

    h = jnp.concatenate([ctx, x], axis=1)
    for i in range(DEPTH):
        kind, j = i % 3, i // 3
        ctx_out = i < DEPTH - 1
        g = norm_g[i]
        if kind == 0:
            lambda_init = 0.8 - 0.6 * math.exp(-0.3 * i)
            q, k, v = _da_pre(h, mods, i, g, w_da_qkv[j].astype(BF16), rope_da)
            y = _da_attn(q, k, v, da_lambda[j], da_subln[j], lambda_init, ctx_out)
            w_out = w_da_out[j]
        elif kind == 1:
            wd, wq, wkv = _mla_weight_layouts(w_mla_down[j], w_mla_uq[j], w_mla_ukv[j])
            q, k, v = _mla_pre(h, mods, i, g, wd, mla_q_norm[j].reshape(1, -1), wq,
                               mla_kv_norm[j].reshape(1, -1), wkv, rope_mla)
            y = _mla_attn(q, k, v, ctx_out)
            w_out = w_mla_out[j]
        else:
            y = _sc_pre(h, mods, i, g, w_sc_in[j].astype(BF16), sc_conv[j])
            w_out = w_sc_out[j]
        h = _post(h, y, mods, i, g, w_out.astype(BF16), w_mlp_in[i].astype(BF16),
                  w_mlp_out[i].astype(BF16), ctx_out)
    return h
```

```python
import functools
import math

import jax
import jax.numpy as jnp
from jax import lax
from jax.experimental import pallas as pl
from jax.experimental.pallas import tpu as pltpu

F32 = jnp.float32
BF16 = jnp.bfloat16

LANES = 128
SUBLANES = 8
VMEM_LIMIT_BYTES = 56 * 1024 * 1024

D = 1024
CTX = 256
SEQ = 2048
NT = CTX + SEQ
DEPTH = 4
N_MOD = 6
MLP_HIDDEN = 4 * D
NORM_EPS = 1e-6
ROPE_THETA = 10000.0
GRID_W = 64

DA_HEAD_DIM = 64
DA_HEADS = 8
DA_SCALE = DA_HEAD_DIM ** -0.5

MLA_HEADS = 16
MLA_Q_LORA = 256
MLA_KV_LORA = 128
MLA_NOPE = 64
MLA_ROPE = 32
MLA_V = 64
MLA_SCALE = (MLA_NOPE + MLA_ROPE) ** -0.5
LOG2E = math.log2(math.e)

SC_WIDTH = 3

TM = CTX
NTILES = NT // TM
MLP_CHUNK = 1024
MOD_ROWS = 24
ATTN_PAIRS = 8
STEP_TILES = 3
STEP_ROWS = STEP_TILES * TM


def _params(n_axes):
    return pltpu.CompilerParams(
        dimension_semantics=("arbitrary",) * n_axes,
        vmem_limit_bytes=VMEM_LIMIT_BYTES,
    )


def _rms(x, g):
    ms = jnp.mean(x * x, axis=-1, keepdims=True)
    return x * lax.rsqrt(ms + NORM_EPS) * g


def _modnorm(h, g, mod):
    return _rms(h, g) * (1.0 + mod[1:2, :]) + mod[0:1, :]


def _softmax_pv(scores, v):
    v_ones = jnp.concatenate([v, jnp.ones_like(v)], axis=1)
    es = [jnp.exp2(s - jnp.max(s, axis=-1, keepdims=True)).astype(BF16) for s in scores]
    os_ = [jnp.dot(e, v_ones, preferred_element_type=F32) for e in es]
    return [o[:, :LANES] / o[:, LANES:] for o in os_]


def _pipelined(n, first_stage, second_stage):
    carried = first_stage(0)
    for i in range(1, n):
        upcoming = first_stage(i)
        second_stage(i - 1, carried)
        carried = upcoming
    second_stage(n - 1, carried)


def _rope_block(x, tabs, sh_hi, sh_lo):
    return x * tabs[0] + pltpu.roll(x, sh_hi, 1) * tabs[1] + pltpu.roll(x, sh_lo, 1) * tabs[2]


def _tile_rows(c):
    return slice(c * TM, (c + 1) * TM)


def _rope_tabs(rope_ref, c):
    return [rope_ref[i, _tile_rows(c), :] for i in range(3)]


def _ada_kernel(c_ref, w_ref, b_ref, o_ref):
    c = c_ref[...]
    s = c / (1.0 + jnp.exp(-c))
    o_ref[...] = jnp.dot(s.astype(BF16), w_ref[...].astype(BF16),
                         preferred_element_type=F32) + b_ref[...]


def _ada_all_layers(cc, w_ada, b_ada):
    return pl.pallas_call(
        _ada_kernel,
        grid=(DEPTH, N_MOD),
        in_specs=[
            pl.BlockSpec((MOD_ROWS, D), lambda l, n: (0, 0)),
            pl.BlockSpec((None, D, D), lambda l, n: (l, 0, n)),
            pl.BlockSpec((None, 1, D), lambda l, n: (l, 0, n)),
        ],
        out_specs=pl.BlockSpec((None, MOD_ROWS, D), lambda l, n: (l, 0, n)),
        out_shape=jax.ShapeDtypeStruct((DEPTH, MOD_ROWS, N_MOD * D), F32),
        compiler_params=_params(2),
        name="ada_mod",
    )(cc, w_ada, b_ada.reshape(DEPTH, 1, N_MOD * D))


def _step_spec(width):
    return pl.BlockSpec((None, STEP_ROWS, width), lambda b, j: (b, j, 0))


def _step_mod_specs(layer, batch):
    def spec(c):
        def index(b, j):
            return (layer, jnp.where(STEP_TILES * j + c == 0, batch, b), 0, 0)
        return pl.BlockSpec((None, None, N_MOD, D), index)
    return [spec(c) for c in range(STEP_TILES)]


def _rope_spec():
    return pl.BlockSpec((3, STEP_ROWS, LANES), lambda b, j: (0, j, 0))


def _full_spec(shape):
    nd = len(shape)
    return pl.BlockSpec(shape, lambda b, t: (0,) * nd, pipeline_mode=pl.Buffered(1))


def _modnorm_tiles(h_ref, g, mod_refs):
    return [_modnorm(h_ref[_tile_rows(c), :], g, mod_refs[c][...]).astype(BF16)
            for c in range(STEP_TILES)]


def _da_pre_kernel(*refs):
    h_ref, mod_refs = refs[0], refs[1:1 + STEP_TILES]
    g_ref, w_ref, rope_ref, q_ref, k_ref, v_ref = refs[1 + STEP_TILES:]
    a = _modnorm_tiles(h_ref, g_ref[0:1, :], mod_refs)
    qkv = [jnp.dot(a_c, w_ref[...], preferred_element_type=F32) for a_c in a]
    for c in range(STEP_TILES):
        rows, tabs = _tile_rows(c), _rope_tabs(rope_ref, c)
        for j in range(D // LANES):
            lo, hi = j * LANES, (j + 1) * LANES
            q = _rope_block(qkv[c][:, lo:hi], tabs, LANES - 32, 32)
            q_ref[rows, lo:hi] = (q * (DA_SCALE * LOG2E)).astype(BF16)
            k = _rope_block(qkv[c][:, D + lo:D + hi], tabs, LANES - 32, 32)
            k_ref[rows, lo:hi] = k.astype(BF16)
        v_ref[rows, :] = qkv[c][:, 2 * D:].astype(BF16)


def _da_pre(h, mods, layer, g, w_qkv, rope):
    b = h.shape[0]
    out = jax.ShapeDtypeStruct((b, NT, D), BF16)
    return pl.pallas_call(
        _da_pre_kernel,
        grid=(b, NT // STEP_ROWS),
        in_specs=[_step_spec(D)] + _step_mod_specs(layer, b)
                 + [_full_spec((4, D)), _full_spec((D, 3 * D)), _rope_spec()],
        out_specs=[_step_spec(D)] * 3,
        out_shape=[out] * 3,
        compiler_params=_params(2),
        name="da_pre",
    )(h, *([mods] * STEP_TILES), g, w_qkv, rope)


def _da_attn_kernel(lam_ref, sub_ref, q_ref, k_ref, v_ref, o_ref, *, lambda_init, first_tile):
    lv = lam_ref[...]
    lam = (jnp.exp(jnp.sum(lv[0:1] * lv[1:2], axis=-1, keepdims=True))
           - jnp.exp(jnp.sum(lv[2:3] * lv[3:4], axis=-1, keepdims=True))
           + lambda_init)

    def attend(nk):
        def scores(p):
            lo, hi = p * LANES, (p + 1) * LANES
            q = q_ref[:, lo:hi]
            lane = lax.broadcasted_iota(jnp.int32, q.shape, 1)
            zero = jnp.zeros_like(q)
            return [lax.dot_general(qm, k_ref[0:nk, lo:hi], (((1,), (1,)), ((), ())),
                                    preferred_element_type=F32)
                    for qm in (jnp.where(lane < DA_HEAD_DIM, q, zero),
                               jnp.where(lane >= DA_HEAD_DIM, q, zero))]

        def finish(p, s):
            lo, hi = p * LANES, (p + 1) * LANES
            pv = _softmax_pv(s, v_ref[0:nk, lo:hi])
            o = pv[0] - lam * pv[1]
            o_ref[:, lo:hi] = (_rms(o, sub_ref[...]) * (1.0 - lambda_init)).astype(BF16)

        _pipelined(ATTN_PAIRS, scores, finish)

    if first_tile == 0:
        t = pl.program_id(2)
        pl.when(t == 0)(lambda: attend(CTX))
        pl.when(t > 0)(lambda: attend(NT))
    else:
        attend(NT)


def _da_attn(q, k, v, lam_vecs, subln, lambda_init, ctx_out):
    b = q.shape[0]
    first_tile = 0 if ctx_out else 1
    width = ATTN_PAIRS * LANES
    kv_spec = pl.BlockSpec((None, NT, width), lambda b_, h, t: (b_, 0, h))
    qo_spec = pl.BlockSpec((None, TM, width), lambda b_, h, t: (b_, t + first_tile, h))
    return pl.pallas_call(
        functools.partial(_da_attn_kernel, lambda_init=lambda_init, first_tile=first_tile),
        grid=(b, DA_HEADS // ATTN_PAIRS, NTILES - first_tile),
        in_specs=[
            pl.BlockSpec((4, DA_HEAD_DIM), lambda b_, h, t: (0, 0)),
            pl.BlockSpec((1, LANES), lambda b_, h, t: (0, 0)),
            qo_spec, kv_spec, kv_spec,
        ],
        out_specs=qo_spec,
        out_shape=jax.ShapeDtypeStruct((b, NT, D), BF16),
        compiler_params=_params(3),
        name="da_attn",
    )(lam_vecs, subln.reshape(1, LANES), q, k, v)


MLA_DOWN_W = 512
MLA_Q_W = MLA_HEADS * LANES
MLA_KV_W = MLA_HEADS * LANES + MLA_HEADS * MLA_V


def _mla_pre_kernel(*refs):
    h_ref, mod_refs = refs[0], refs[1:1 + STEP_TILES]
    (g_ref, wd_ref, qg_ref, wq_ref, kg_ref, wkv_ref, rope_ref,
     q_ref, k_ref, v_ref) = refs[1 + STEP_TILES:]
    tiles = range(STEP_TILES)
    a = _modnorm_tiles(h_ref, g_ref[0:1, :], mod_refs)
    d = [jnp.dot(a_c, wd_ref[...], preferred_element_type=F32) for a_c in a]
    cq = [_rms(d_c[:, :MLA_Q_LORA], qg_ref[...]).astype(BF16) for d_c in d]
    ckv = [_rms(d_c[:, MLA_Q_LORA:MLA_Q_LORA + MLA_KV_LORA], kg_ref[...]).astype(BF16) for d_c in d]
    q = [jnp.dot(cq[c], wq_ref[...], preferred_element_type=F32) for c in tiles]
    kv = [jnp.dot(ckv[c], wkv_ref[...], preferred_element_type=F32) for c in tiles]
    for c in tiles:
        rows, tabs = _tile_rows(c), _rope_tabs(rope_ref, c)
        kr = _rope_block(d[c][:, MLA_Q_LORA + MLA_KV_LORA:], tabs, LANES - 16, 16)
        for j in range(MLA_HEADS):
            lo, hi = j * LANES, (j + 1) * LANES
            qj = _rope_block(q[c][:, lo:hi], tabs, LANES - 16, 16)
            q_ref[rows, lo:hi] = (qj * (MLA_SCALE * LOG2E)).astype(BF16)
            k_ref[rows, lo:hi] = (kv[c][:, lo:hi] + kr).astype(BF16)
        v_ref[rows, :] = kv[c][:, MLA_HEADS * LANES:].astype(BF16)


def _mla_pre(h, mods, layer, g, wd, qg, wq, kg, wkv, rope):
    b = h.shape[0]
    return pl.pallas_call(
        _mla_pre_kernel,
        grid=(b, NT // STEP_ROWS),
        in_specs=[_step_spec(D)] + _step_mod_specs(layer, b) + [
            _full_spec((4, D)),
            _full_spec((D, MLA_DOWN_W)),
            _full_spec((1, MLA_Q_LORA)),
            _full_spec((MLA_Q_LORA, MLA_Q_W)),
            _full_spec((1, MLA_KV_LORA)),
            _full_spec((MLA_KV_LORA, MLA_KV_W)),
            _rope_spec(),
        ],
        out_specs=[_step_spec(MLA_Q_W), _step_spec(MLA_Q_W), _step_spec(D)],
        out_shape=[jax.ShapeDtypeStruct((b, NT, MLA_Q_W), BF16),
                   jax.ShapeDtypeStruct((b, NT, MLA_Q_W), BF16),
                   jax.ShapeDtypeStruct((b, NT, D), BF16)],
        compiler_params=_params(2),
        name="mla_pre",
    )(h, *([mods] * STEP_TILES), g, wd, qg, wq, kg, wkv, rope)


def _mla_attn_kernel(q_ref, k_ref, v_ref, o_ref, *, first_tile):
    def attend(nk):
        def scores(p):
            return [lax.dot_general(q_ref[:, h * LANES:(h + 1) * LANES],
                                    k_ref[0:nk, h * LANES:(h + 1) * LANES],
                                    (((1,), (1,)), ((), ())), preferred_element_type=F32)
                    for h in (2 * p, 2 * p + 1)]

        def finish(p, s):
            pv = _softmax_pv(s, v_ref[0:nk, p * LANES:(p + 1) * LANES])
            lane = lax.broadcasted_iota(jnp.int32, (TM, LANES), 1)
            o_ref[:, p * LANES:(p + 1) * LANES] = jnp.where(lane < MLA_V, pv[0], pv[1]).astype(BF16)

        _pipelined(ATTN_PAIRS, scores, finish)

    if first_tile == 0:
        t = pl.program_id(2)
        pl.when(t == 0)(lambda: attend(CTX))
        pl.when(t > 0)(lambda: attend(NT))
    else:
        attend(NT)


def _mla_attn(q, k, v, ctx_out):
    b = q.shape[0]
    first_tile = 0 if ctx_out else 1
    steps = MLA_HEADS // (2 * ATTN_PAIRS)
    width = ATTN_PAIRS * LANES
    return pl.pallas_call(
        functools.partial(_mla_attn_kernel, first_tile=first_tile),
        grid=(b, steps, NTILES - first_tile),
        in_specs=[
            pl.BlockSpec((None, TM, 2 * width), lambda b_, h, t: (b_, t + first_tile, h)),
            pl.BlockSpec((None, NT, 2 * width), lambda b_, h, t: (b_, 0, h)),
            pl.BlockSpec((None, NT, width), lambda b_, h, t: (b_, 0, h)),
        ],
        out_specs=pl.BlockSpec((None, TM, width), lambda b_, h, t: (b_, t + first_tile, h)),
        out_shape=jax.ShapeDtypeStruct((b, NT, D), BF16),
        compiler_params=_params(3),
        name="mla_attn",
    )(q, k, v)


def _sc_pre_kernel(*refs):
    hp_ref, h_ref, hn_ref, mod_refs = refs[0], refs[1], refs[2], refs[3:3 + STEP_TILES]
    g_ref, w_ref, cw_ref, y_ref = refs[3 + STEP_TILES:]
    step = pl.program_id(1)
    tiles = range(STEP_TILES)
    halo = SUBLANES
    rows = jnp.concatenate([hp_ref[...], h_ref[...], hn_ref[...]], axis=0)
    a = [_modnorm(rows[c * TM:(c + 1) * TM + 2 * halo], g_ref[0:1, :], mod_refs[c][...]).astype(BF16)
         for c in tiles]
    p = [jnp.dot(a_c, w_ref[...], preferred_element_type=F32) for a_c in a]
    row = lax.broadcasted_iota(jnp.int32, (TM, 1), 0)
    for c in tiles:
        tile = STEP_TILES * step + c
        z = p[c][:, D:2 * D] * p[c][:, 2 * D:]
        b_gate = p[c][halo:halo + TM, :D]
        first = jnp.logical_and(row == 0, tile <= 1)
        last = jnp.logical_and(row == TM - 1, jnp.logical_or(tile == 0, tile == NTILES - 1))
        z_prev = jnp.where(first, 0.0, z[halo - 1:halo - 1 + TM])
        z_next = jnp.where(last, 0.0, z[halo + 1:halo + 1 + TM])
        u = cw_ref[0:1, :] * z_prev + cw_ref[1:2, :] * z[halo:halo + TM] + cw_ref[2:3, :] * z_next
        y_ref[_tile_rows(c), :] = (b_gate * u).astype(BF16)


def _sc_pre(h, mods, layer, g, w_in, conv_w):
    b = h.shape[0]
    per_step = STEP_ROWS // SUBLANES
    last_halo = NT // SUBLANES - 1
    return pl.pallas_call(
        _sc_pre_kernel,
        grid=(b, NT // STEP_ROWS),
        in_specs=[
            pl.BlockSpec((None, SUBLANES, D),
                         lambda b_, j: (b_, jnp.maximum(j * per_step - 1, 0), 0)),
            _step_spec(D),
            pl.BlockSpec((None, SUBLANES, D),
                         lambda b_, j: (b_, jnp.minimum((j + 1) * per_step, last_halo), 0)),
        ] + _step_mod_specs(layer, b) + [
            _full_spec((4, D)),
            _full_spec((D, 3 * D)),
            _full_spec((SC_WIDTH, D)),
        ],
        out_specs=_step_spec(D),
        out_shape=jax.ShapeDtypeStruct((b, NT, D), BF16),
        compiler_params=_params(2),
        name="sc_pre",
    )(h, h, h, *([mods] * STEP_TILES), g, w_in, conv_w)


def _post_kernel(*refs, tiles):
    h_refs, y_refs, mod_refs = refs[:tiles], refs[tiles:2 * tiles], refs[2 * tiles:3 * tiles]
    g_ref, wo_ref, w1_ref, w2_ref, o_ref = refs[3 * tiles:]
    r = range(tiles)
    mod = [mod_refs[c][...] for c in r]
    y = [jnp.dot(y_refs[c][...], wo_ref[...], preferred_element_type=F32) for c in r]
    h = [h_refs[c][...] + mod[c][2:3, :] * _rms(y[c], g_ref[1:2, :]) for c in r]
    a = [(_rms(h[c], g_ref[2:3, :]) * (1.0 + mod[c][4:5, :]) + mod[c][3:4, :]).astype(BF16) for c in r]
    f = [jnp.zeros((TM, D), F32) for c in r]
    for m in range(MLP_HIDDEN // MLP_CHUNK):
        lo, hi = m * MLP_CHUNK, (m + 1) * MLP_CHUNK
        u = [jnp.maximum(jnp.dot(a[c], w1_ref[:, lo:hi], preferred_element_type=F32), 0.0) for c in r]
        f = [f[c] + jnp.dot((u[c] * u[c]).astype(BF16), w2_ref[lo:hi, :], preferred_element_type=F32)
             for c in r]
    for c in r:
        o_ref[_tile_rows(c), :] = h[c] + mod[c][5:6, :] * _rms(f[c], g_ref[3:4, :])


def _post(h, y, mods, layer, g, w_out, w1, w2, ctx_out):
    b = h.shape[0]
    first_tile = 0 if ctx_out else 1
    n_tiles = NTILES - first_tile
    tiles = 3 if n_tiles % 3 == 0 else 2
    assert n_tiles % tiles == 0

    def tile_spec(c):
        return pl.BlockSpec((None, TM, D), lambda b_, j: (b_, tiles * j + c + first_tile, 0))

    def mod_spec(c):
        def index(b_, j):
            return (layer, jnp.where(tiles * j + c + first_tile == 0, b, b_), 0, 0)
        return pl.BlockSpec((None, None, N_MOD, D), index)

    chain = range(tiles)
    return pl.pallas_call(
        functools.partial(_post_kernel, tiles=tiles),
        grid=(b, n_tiles // tiles),
        in_specs=[tile_spec(c) for c in chain] + [tile_spec(c) for c in chain]
                 + [mod_spec(c) for c in chain]
                 + [_full_spec((4, D)), _full_spec((D, D)), _full_spec((D, MLP_HIDDEN)),
                    _full_spec((MLP_HIDDEN, D))],
        out_specs=pl.BlockSpec((None, tiles * TM, D), lambda b_, j: (b_, j, 0)),
        out_shape=jax.ShapeDtypeStruct((b, n_tiles * TM, D), F32),
        compiler_params=_params(2),
        name="post_mlp",
    )(*([h] * tiles), *([y] * tiles), *([mods] * tiles), g, w_out, w1, w2)


def _rope_tables(rot_dim, lane0, chunk):
    quarter = rot_dim // 4
    half = rot_dim // 2
    pos = jnp.arange(SEQ, dtype=F32)
    row = jnp.floor(pos / GRID_W)
    col = pos - row * GRID_W
    inv_freq = ROPE_THETA ** (-jnp.arange(quarter, dtype=F32) / quarter)
    ang = jnp.concatenate([row[:, None] * inv_freq, col[:, None] * inv_freq], axis=-1)
    cos, sin = jnp.cos(ang), jnp.sin(ang)
    lane = jnp.arange(LANES)
    off = (lane % chunk) - lane0
    in_rot = (off >= 0) & (off < rot_dim)
    first = in_rot & (off < half)
    second = in_rot & (off >= half)
    idx = jnp.clip(off % half, 0, half - 1)
    c_tab = jnp.where(in_rot[None, :], cos[:, idx], 1.0)
    s_hi = jnp.where(first[None, :], -sin[:, idx], 0.0)
    s_lo = jnp.where(second[None, :], sin[:, idx], 0.0)
    lat = jnp.stack([c_tab, s_hi, s_lo])
    ctx = jnp.stack([jnp.ones((CTX, LANES), F32), jnp.zeros((CTX, LANES), F32),
                     jnp.zeros((CTX, LANES), F32)])
    return jnp.concatenate([ctx, lat], axis=1)


def _mla_weight_layouts(w_down, w_uq, w_ukv):
    zeros = functools.partial(jnp.zeros, dtype=w_down.dtype)
    n_lat = MLA_Q_LORA + MLA_KV_LORA
    wd = jnp.concatenate([w_down[:, :n_lat], zeros((D, MLA_NOPE)), w_down[:, n_lat:],
                          zeros((D, LANES - MLA_NOPE - MLA_ROPE))], axis=1)
    wq = w_uq.reshape(MLA_Q_LORA, MLA_HEADS, MLA_NOPE + MLA_ROPE)
    wq = jnp.pad(wq, ((0, 0), (0, 0), (0, LANES - MLA_NOPE - MLA_ROPE))).reshape(MLA_Q_LORA, MLA_Q_W)
    wkv = w_ukv.reshape(MLA_KV_LORA, MLA_HEADS, MLA_NOPE + MLA_V)
    wk = jnp.pad(wkv[..., :MLA_NOPE], ((0, 0), (0, 0), (0, LANES - MLA_NOPE)))
    wkv = jnp.concatenate([wk.reshape(MLA_KV_LORA, MLA_HEADS * LANES),
                           wkv[..., MLA_NOPE:].reshape(MLA_KV_LORA, MLA_HEADS * MLA_V)], axis=1)
    return wd.astype(BF16), wq.astype(BF16), wkv.astype(BF16)


def kernel(x, c, ctx, c_ctx, w_ada, b_ada, norm_g, w_mlp_in, w_mlp_out, w_da_qkv, da_lambda, da_subln, w_da_out, w_mla_down, mla_q_norm, w_mla_uq, mla_kv_norm, w_mla_ukv, w_mla_out, w_sc_in, sc_conv, w_sc_out):
    b = x.shape[0]
    assert x.shape == (b, SEQ, D) and ctx.shape == (b, CTX, D) and b + 1 <= MOD_ROWS

    cc = jnp.concatenate([c, c_ctx[None, :], jnp.zeros((MOD_ROWS - b - 1, D), F32)], axis=0)
    mods = _ada_all_layers(cc, w_ada, b_ada).reshape(DEPTH, MOD_ROWS, N_MOD, D)

    rope_da = _rope_tables(DA_HEAD_DIM, 0, DA_HEAD_DIM)
    rope_mla = _rope_tables(MLA_ROPE, MLA_NOPE, LANES)

    h = jnp.concatenate([ctx, x], axis=1)
    for i in range(DEPTH):
        kind, j = i % 3, i // 3
        ctx_out = i < DEPTH - 1
        g = norm_g[i]
        if kind == 0:
            lambda_init = 0.8 - 0.6 * math.exp(-0.3 * i)
            q, k, v = _da_pre(h, mods, i, g, w_da_qkv[j].astype(BF16), rope_da)
            y = _da_attn(q, k, v, da_lambda[j], da_subln[j], lambda_init, ctx_out)
            w_out = w_da_out[j]
        elif kind == 1:
            wd, wq, wkv = _mla_weight_layouts(w_mla_down[j], w_mla_uq[j], w_mla_ukv[j])
            q, k, v = _mla_pre(h, mods, i, g, wd, mla_q_norm[j].reshape(1, -1), wq,
                               mla_kv_norm[j].reshape(1, -1), wkv, rope_mla)
            y = _mla_attn(q, k, v, ctx_out)
            w_out = w_mla_out[j]
        else:
            y = _sc_pre(h, mods, i, g, w_sc_in[j].astype(BF16), sc_conv[j])
            w_out = w_sc_out[j]
        h = _post(h, y, mods, i, g, w_out.astype(BF16), w_mlp_in[i].astype(BF16),
                  w_mlp_out[i].astype(BF16), ctx_out)
    return h
```

```python
import functools
import math

import jax
import jax.numpy as jnp
from jax import lax
from jax.experimental import pallas as pl
from jax.experimental.pallas import tpu as pltpu

F32 = jnp.float32
BF16 = jnp.bfloat16

LANES = 128
SUBLANES = 8
VMEM_LIMIT_BYTES = 56 * 1024 * 1024

D = 1024
CTX = 256
SEQ = 2048
NT = CTX + SEQ
DEPTH = 4
N_MOD = 6
MLP_HIDDEN = 4 * D
NORM_EPS = 1e-6
ROPE_THETA = 10000.0
GRID_W = 64

DA_HEAD_DIM = 64
DA_HEADS = 8
DA_SCALE = DA_HEAD_DIM ** -0.5

MLA_HEADS = 16
MLA_Q_LORA = 256
MLA_KV_LORA = 128
MLA_NOPE = 64
MLA_ROPE = 32
MLA_V = 64
MLA_SCALE = (MLA_NOPE + MLA_ROPE) ** -0.5
LOG2E = math.log2(math.e)

SC_WIDTH = 3

TM = CTX
NTILES = NT // TM
MLP_CHUNK = 1024
MOD_ROWS = 24
ATTN_PAIRS = 8
STEP_TILES = 3
STEP_ROWS = STEP_TILES * TM


def _params(n_axes):
    return pltpu.CompilerParams(
        dimension_semantics=("arbitrary",) * n_axes,
        vmem_limit_bytes=VMEM_LIMIT_BYTES,
    )


def _rms(x, g):
    ms = jnp.mean(x * x, axis=-1, keepdims=True)
    return x * lax.rsqrt(ms + NORM_EPS) * g


def _modnorm(h, g, mod):
    return _rms(h, g) * (1.0 + mod[1:2, :]) + mod[0:1, :]


def _softmax_pv(scores, v):
    v_ones = jnp.concatenate([v, jnp.ones_like(v)], axis=1)
    es = [jnp.exp2(s - jnp.max(s, axis=-1, keepdims=True)).astype(BF16) for s in scores]
    os_ = [jnp.dot(e, v_ones, preferred_element_type=F32) for e in es]
    return [o[:, :LANES] / o[:, LANES:] for o in os_]


def _pipelined(n, first_stage, second_stage):
    carried = first_stage(0)
    for i in range(1, n):
        upcoming = first_stage(i)
        second_stage(i - 1, carried)
        carried = upcoming
    second_stage(n - 1, carried)


def _rope_block(x, tabs, sh_hi, sh_lo):
    return x * tabs[0] + pltpu.roll(x, sh_hi, 1) * tabs[1] + pltpu.roll(x, sh_lo, 1) * tabs[2]


def _tile_rows(c):
    return slice(c * TM, (c + 1) * TM)


def _rope_tabs(rope_ref, c):
    return [rope_ref[i, _tile_rows(c), :] for i in range(3)]


def _ada_kernel(c_ref, w_ref, b_ref, o_ref):
    c = c_ref[...]
    s = c / (1.0 + jnp.exp(-c))
    o_ref[...] = jnp.dot(s.astype(BF16), w_ref[...].astype(BF16),
                         preferred_element_type=F32) + b_ref[...]


def _ada_all_layers(cc, w_ada, b_ada):
    return pl.pallas_call(
        _ada_kernel,
        grid=(DEPTH, N_MOD),
        in_specs=[
            pl.BlockSpec((MOD_ROWS, D), lambda l, n: (0, 0)),
            pl.BlockSpec((None, D, D), lambda l, n: (l, 0, n)),
            pl.BlockSpec((None, 1, D), lambda l, n: (l, 0, n)),
        ],
        out_specs=pl.BlockSpec((None, MOD_ROWS, D), lambda l, n: (l, 0, n)),
        out_shape=jax.ShapeDtypeStruct((DEPTH, MOD_ROWS, N_MOD * D), F32),
        compiler_params=_params(2),
        name="ada_mod",
    )(cc, w_ada, b_ada.reshape(DEPTH, 1, N_MOD * D))


def _step_spec(width):
    return pl.BlockSpec((None, STEP_ROWS, width), lambda b, j: (b, j, 0))


def _step_mod_specs(layer, batch):
    def spec(c):
        def index(b, j):
            return (layer, jnp.where(STEP_TILES * j + c == 0, batch, b), 0, 0)
        return pl.BlockSpec((None, None, N_MOD, D), index)
    return [spec(c) for c in range(STEP_TILES)]


def _rope_spec():
    return pl.BlockSpec((3, STEP_ROWS, LANES), lambda b, j: (0, j, 0))


def _full_spec(shape):
    nd = len(shape)
    return pl.BlockSpec(shape, lambda b, t: (0,) * nd, pipeline_mode=pl.Buffered(1))


def _modnorm_tiles(h_ref, g, mod_refs):
    return [_modnorm(h_ref[_tile_rows(c), :], g, mod_refs[c][...]).astype(BF16)
            for c in range(STEP_TILES)]


def _layer_spec(shape, layer):
    nd = len(shape)
    return pl.BlockSpec((None,) + shape, lambda b, t: (layer,) + (0,) * nd,
                        pipeline_mode=pl.Buffered(1))


def _split_stream_specs(tiles, first_tile):
    def lat_spec(c):
        return pl.BlockSpec(
            (None, TM, D), lambda b, j: (b, jnp.maximum(tiles * j + c + first_tile - 1, 0), 0))
    return [pl.BlockSpec((None, TM, D), lambda b, j: (b, 0, 0))] + [lat_spec(c) for c in range(tiles)]


def _split_stream_tiles(ctx_ref, lat_refs):
    first = jnp.where(pl.program_id(1) == 0, ctx_ref[...], lat_refs[0][...])
    return [first] + [r[...] for r in lat_refs[1:]]


def _da_pre_kernel(*refs, split):
    n_in = 1 + STEP_TILES if split else 1
    mod_refs = refs[n_in:n_in + STEP_TILES]
    g_ref, w_ref, rope_ref, q_ref, k_ref, v_ref = refs[n_in + STEP_TILES:]
    if split:
        h = _split_stream_tiles(refs[0], refs[1:n_in])
        a = [_modnorm(h[c], g_ref[0:1, :], mod_refs[c][...]).astype(BF16) for c in range(STEP_TILES)]
    else:
        a = _modnorm_tiles(refs[0], g_ref[0:1, :], mod_refs)
    qkv = [jnp.dot(a_c, w_ref[...], preferred_element_type=F32) for a_c in a]
    for c in range(STEP_TILES):
        rows, tabs = _tile_rows(c), _rope_tabs(rope_ref, c)
        for j in range(D // LANES):
            lo, hi = j * LANES, (j + 1) * LANES
            q = _rope_block(qkv[c][:, lo:hi], tabs, LANES - 32, 32)
            q_ref[rows, lo:hi] = (q * (DA_SCALE * LOG2E)).astype(BF16)
            k = _rope_block(qkv[c][:, D + lo:D + hi], tabs, LANES - 32, 32)
            k_ref[rows, lo:hi] = k.astype(BF16)
        v_ref[rows, :] = qkv[c][:, 2 * D:].astype(BF16)


def _da_pre(h, mods, layer, norm_g, w_qkv, j, rope):
    split = isinstance(h, tuple)
    streams = (h[0],) + (h[1],) * STEP_TILES if split else (h,)
    b = streams[0].shape[0]
    out = jax.ShapeDtypeStruct((b, NT, D), BF16)
    return pl.pallas_call(
        functools.partial(_da_pre_kernel, split=split),
        grid=(b, NT // STEP_ROWS),
        in_specs=(_split_stream_specs(STEP_TILES, 0) if split else [_step_spec(D)])
                 + _step_mod_specs(layer, b)
                 + [_layer_spec((4, D), layer), _layer_spec((D, 3 * D), j), _rope_spec()],
        out_specs=[_step_spec(D)] * 3,
        out_shape=[out] * 3,
        compiler_params=_params(2),
        name="da_pre",
    )(*streams, *([mods] * STEP_TILES), norm_g, w_qkv, rope)


def _da_attn_kernel(lam_ref, sub_ref, q_ref, k_ref, v_ref, o_ref, *, lambda_init, first_tile):
    lv = lam_ref[...]
    lam = (jnp.exp(jnp.sum(lv[0:1] * lv[1:2], axis=-1, keepdims=True))
           - jnp.exp(jnp.sum(lv[2:3] * lv[3:4], axis=-1, keepdims=True))
           + lambda_init)

    def attend(nk):
        def scores(p):
            lo, hi = p * LANES, (p + 1) * LANES
            q = q_ref[:, lo:hi]
            lane = lax.broadcasted_iota(jnp.int32, q.shape, 1)
            zero = jnp.zeros_like(q)
            return [lax.dot_general(qm, k_ref[0:nk, lo:hi], (((1,), (1,)), ((), ())),
                                    preferred_element_type=F32)
                    for qm in (jnp.where(lane < DA_HEAD_DIM, q, zero),
                               jnp.where(lane >= DA_HEAD_DIM, q, zero))]

        def finish(p, s):
            lo, hi = p * LANES, (p + 1) * LANES
            pv = _softmax_pv(s, v_ref[0:nk, lo:hi])
            o = pv[0] - lam * pv[1]
            o_ref[:, lo:hi] = (_rms(o, sub_ref[...]) * (1.0 - lambda_init)).astype(BF16)

        _pipelined(ATTN_PAIRS, scores, finish)

    if first_tile == 0:
        t = pl.program_id(2)
        pl.when(t == 0)(lambda: attend(CTX))
        pl.when(t > 0)(lambda: attend(NT))
    else:
        attend(NT)


def _da_attn(q, k, v, lam_vecs, subln, lambda_init, ctx_out):
    b = q.shape[0]
    first_tile = 0 if ctx_out else 1
    width = ATTN_PAIRS * LANES
    kv_spec = pl.BlockSpec((None, NT, width), lambda b_, h, t: (b_, 0, h))
    qo_spec = pl.BlockSpec((None, TM, width), lambda b_, h, t: (b_, t + first_tile, h))
    return pl.pallas_call(
        functools.partial(_da_attn_kernel, lambda_init=lambda_init, first_tile=first_tile),
        grid=(b, DA_HEADS // ATTN_PAIRS, NTILES - first_tile),
        in_specs=[
            pl.BlockSpec((4, DA_HEAD_DIM), lambda b_, h, t: (0, 0)),
            pl.BlockSpec((1, LANES), lambda b_, h, t: (0, 0)),
            qo_spec, kv_spec, kv_spec,
        ],
        out_specs=qo_spec,
        out_shape=jax.ShapeDtypeStruct((b, NT, D), BF16),
        compiler_params=_params(3),
        name="da_attn",
    )(lam_vecs, subln.reshape(1, LANES), q, k, v)


MLA_DOWN_W = 512
MLA_Q_W = MLA_HEADS * LANES
MLA_KV_W = MLA_HEADS * LANES + MLA_HEADS * MLA_V


def _mla_pre_kernel(*refs):
    h_ref, mod_refs = refs[0], refs[1:1 + STEP_TILES]
    (g_ref, wd_ref, qg_ref, wq_ref, kg_ref, wkv_ref, rope_ref,
     q_ref, k_ref, v_ref) = refs[1 + STEP_TILES:]
    tiles = range(STEP_TILES)
    a = _modnorm_tiles(h_ref, g_ref[0:1, :], mod_refs)
    d = [jnp.dot(a_c, wd_ref[...], preferred_element_type=F32) for a_c in a]
    cq = [_rms(d_c[:, :MLA_Q_LORA], qg_ref[...]).astype(BF16) for d_c in d]
    ckv = [_rms(d_c[:, MLA_Q_LORA:MLA_Q_LORA + MLA_KV_LORA], kg_ref[...]).astype(BF16) for d_c in d]
    q = [jnp.dot(cq[c], wq_ref[...], preferred_element_type=F32) for c in tiles]
    kv = [jnp.dot(ckv[c], wkv_ref[...], preferred_element_type=F32) for c in tiles]
    for c in tiles:
        rows, tabs = _tile_rows(c), _rope_tabs(rope_ref, c)
        kr = _rope_block(d[c][:, MLA_Q_LORA + MLA_KV_LORA:], tabs, LANES - 16, 16)
        for j in range(MLA_HEADS):
            lo, hi = j * LANES, (j + 1) * LANES
            qj = _rope_block(q[c][:, lo:hi], tabs, LANES - 16, 16)
            q_ref[rows, lo:hi] = (qj * (MLA_SCALE * LOG2E)).astype(BF16)
            k_ref[rows, lo:hi] = (kv[c][:, lo:hi] + kr).astype(BF16)
        v_ref[rows, :] = kv[c][:, MLA_HEADS * LANES:].astype(BF16)


def _mla_pre(h, mods, layer, norm_g, wd, qg, wq, kg, wkv, rope):
    b = h.shape[0]
    return pl.pallas_call(
        _mla_pre_kernel,
        grid=(b, NT // STEP_ROWS),
        in_specs=[_step_spec(D)] + _step_mod_specs(layer, b) + [
            _layer_spec((4, D), layer),
            _full_spec((D, MLA_DOWN_W)),
            _full_spec((1, MLA_Q_LORA)),
            _full_spec((MLA_Q_LORA, MLA_Q_W)),
            _full_spec((1, MLA_KV_LORA)),
            _full_spec((MLA_KV_LORA, MLA_KV_W)),
            _rope_spec(),
        ],
        out_specs=[_step_spec(MLA_Q_W), _step_spec(MLA_Q_W), _step_spec(D)],
        out_shape=[jax.ShapeDtypeStruct((b, NT, MLA_Q_W), BF16),
                   jax.ShapeDtypeStruct((b, NT, MLA_Q_W), BF16),
                   jax.ShapeDtypeStruct((b, NT, D), BF16)],
        compiler_params=_params(2),
        name="mla_pre",
    )(h, *([mods] * STEP_TILES), norm_g, wd, qg, wq, kg, wkv, rope)


def _mla_attn_kernel(q_ref, k_ref, v_ref, o_ref, *, first_tile):
    def attend(nk):
        def scores(p):
            return [lax.dot_general(q_ref[:, h * LANES:(h + 1) * LANES],
                                    k_ref[0:nk, h * LANES:(h + 1) * LANES],
                                    (((1,), (1,)), ((), ())), preferred_element_type=F32)
                    for h in (2 * p, 2 * p + 1)]

        def finish(p, s):
            pv = _softmax_pv(s, v_ref[0:nk, p * LANES:(p + 1) * LANES])
            lane = lax.broadcasted_iota(jnp.int32, (TM, LANES), 1)
            o_ref[:, p * LANES:(p + 1) * LANES] = jnp.where(lane < MLA_V, pv[0], pv[1]).astype(BF16)

        _pipelined(ATTN_PAIRS, scores, finish)

    if first_tile == 0:
        t = pl.program_id(2)
        pl.when(t == 0)(lambda: attend(CTX))
        pl.when(t > 0)(lambda: attend(NT))
    else:
        attend(NT)


def _mla_attn(q, k, v, ctx_out):
    b = q.shape[0]
    first_tile = 0 if ctx_out else 1
    steps = MLA_HEADS // (2 * ATTN_PAIRS)
    width = ATTN_PAIRS * LANES
    return pl.pallas_call(
        functools.partial(_mla_attn_kernel, first_tile=first_tile),
        grid=(b, steps, NTILES - first_tile),
        in_specs=[
            pl.BlockSpec((None, TM, 2 * width), lambda b_, h, t: (b_, t + first_tile, h)),
            pl.BlockSpec((None, NT, 2 * width), lambda b_, h, t: (b_, 0, h)),
            pl.BlockSpec((None, NT, width), lambda b_, h, t: (b_, 0, h)),
        ],
        out_specs=pl.BlockSpec((None, TM, width), lambda b_, h, t: (b_, t + first_tile, h)),
        out_shape=jax.ShapeDtypeStruct((b, NT, D), BF16),
        compiler_params=_params(3),
        name="mla_attn",
    )(q, k, v)


def _sc_pre_kernel(*refs):
    hp_ref, h_ref, hn_ref, mod_refs = refs[0], refs[1], refs[2], refs[3:3 + STEP_TILES]
    g_ref, w_ref, cw_ref, y_ref = refs[3 + STEP_TILES:]
    step = pl.program_id(1)
    tiles = range(STEP_TILES)
    halo = SUBLANES
    rows = jnp.concatenate([hp_ref[...], h_ref[...], hn_ref[...]], axis=0)
    a = [_modnorm(rows[c * TM:(c + 1) * TM + 2 * halo], g_ref[0:1, :], mod_refs[c][...]).astype(BF16)
         for c in tiles]
    p = [jnp.dot(a_c, w_ref[...], preferred_element_type=F32) for a_c in a]
    row = lax.broadcasted_iota(jnp.int32, (TM, 1), 0)
    for c in tiles:
        tile = STEP_TILES * step + c
        z = p[c][:, D:2 * D] * p[c][:, 2 * D:]
        b_gate = p[c][halo:halo + TM, :D]
        first = jnp.logical_and(row == 0, tile <= 1)
        last = jnp.logical_and(row == TM - 1, jnp.logical_or(tile == 0, tile == NTILES - 1))
        z_prev = jnp.where(first, 0.0, z[halo - 1:halo - 1 + TM])
        z_next = jnp.where(last, 0.0, z[halo + 1:halo + 1 + TM])
        u = cw_ref[0:1, :] * z_prev + cw_ref[1:2, :] * z[halo:halo + TM] + cw_ref[2:3, :] * z_next
        y_ref[_tile_rows(c), :] = (b_gate * u).astype(BF16)


def _sc_pre(h, mods, layer, norm_g, w_in, conv_w, j):
    b = h.shape[0]
    per_step = STEP_ROWS // SUBLANES
    last_halo = NT // SUBLANES - 1
    return pl.pallas_call(
        _sc_pre_kernel,
        grid=(b, NT // STEP_ROWS),
        in_specs=[
            pl.BlockSpec((None, SUBLANES, D),
                         lambda b_, j: (b_, jnp.maximum(j * per_step - 1, 0), 0)),
            _step_spec(D),
            pl.BlockSpec((None, SUBLANES, D),
                         lambda b_, j: (b_, jnp.minimum((j + 1) * per_step, last_halo), 0)),
        ] + _step_mod_specs(layer, b) + [
            _layer_spec((4, D), layer),
            _layer_spec((D, 3 * D), j),
            _layer_spec((SC_WIDTH, D), j),
        ],
        out_specs=_step_spec(D),
        out_shape=jax.ShapeDtypeStruct((b, NT, D), BF16),
        compiler_params=_params(2),
        name="sc_pre",
    )(h, h, h, *([mods] * STEP_TILES), norm_g, w_in, conv_w)


def _post_kernel(*refs, tiles, split):
    if split:
        h_in, refs = _split_stream_tiles(refs[0], refs[1:1 + tiles]), refs[1 + tiles:]
    else:
        h_in, refs = [ref[...] for ref in refs[:tiles]], refs[tiles:]
    y_refs, mod_refs = refs[:tiles], refs[tiles:2 * tiles]
    g_ref, wo_ref, w1_ref, w2_ref, o_ref = refs[2 * tiles:]
    r = range(tiles)
    mod = [mod_refs[c][...] for c in r]
    y = [jnp.dot(y_refs[c][...], wo_ref[...], preferred_element_type=F32) for c in r]
    h = [h_in[c] + mod[c][2:3, :] * _rms(y[c], g_ref[1:2, :]) for c in r]
    a = [(_rms(h[c], g_ref[2:3, :]) * (1.0 + mod[c][4:5, :]) + mod[c][3:4, :]).astype(BF16) for c in r]
    f = [jnp.zeros((TM, D), F32) for c in r]
    for m in range(MLP_HIDDEN // MLP_CHUNK):
        lo, hi = m * MLP_CHUNK, (m + 1) * MLP_CHUNK
        u = [jnp.maximum(jnp.dot(a[c], w1_ref[:, lo:hi], preferred_element_type=F32), 0.0) for c in r]
        f = [f[c] + jnp.dot((u[c] * u[c]).astype(BF16), w2_ref[lo:hi, :], preferred_element_type=F32)
             for c in r]
    for c in r:
        o_ref[_tile_rows(c), :] = h[c] + mod[c][5:6, :] * _rms(f[c], g_ref[3:4, :])


def _post(h, y, mods, layer, norm_g, w_out, j, w1, w2, ctx_out):
    split = isinstance(h, tuple)
    b = y.shape[0]
    first_tile = 0 if ctx_out else 1
    n_tiles = NTILES - first_tile
    tiles = 3 if n_tiles % 3 == 0 else 2
    assert n_tiles % tiles == 0

    def tile_spec(c):
        return pl.BlockSpec((None, TM, D), lambda b_, s: (b_, tiles * s + c + first_tile, 0))

    def mod_spec(c):
        def index(b_, s):
            return (layer, jnp.where(tiles * s + c + first_tile == 0, b, b_), 0, 0)
        return pl.BlockSpec((None, None, N_MOD, D), index)

    chain = range(tiles)
    streams = (h[0],) + (h[1],) * tiles if split else (h,) * tiles
    return pl.pallas_call(
        functools.partial(_post_kernel, tiles=tiles, split=split),
        grid=(b, n_tiles // tiles),
        in_specs=(_split_stream_specs(tiles, first_tile) if split else [tile_spec(c) for c in chain])
                 + [tile_spec(c) for c in chain] + [mod_spec(c) for c in chain]
                 + [_layer_spec((4, D), layer), _layer_spec((D, D), j),
                    _layer_spec((D, MLP_HIDDEN), layer), _layer_spec((MLP_HIDDEN, D), layer)],
        out_specs=pl.BlockSpec((None, tiles * TM, D), lambda b_, s: (b_, s, 0)),
        out_shape=jax.ShapeDtypeStruct((b, n_tiles * TM, D), F32),
        compiler_params=_params(2),
        name="post_mlp",
    )(*streams, *([y] * tiles), *([mods] * tiles), norm_g, w_out, w1, w2)


def _rope_tables(rot_dim, lane0, chunk):
    quarter = rot_dim // 4
    half = rot_dim // 2
    pos = jnp.arange(SEQ, dtype=F32)
    row = jnp.floor(pos / GRID_W)
    col = pos - row * GRID_W
    inv_freq = ROPE_THETA ** (-jnp.arange(quarter, dtype=F32) / quarter)
    ang = jnp.concatenate([row[:, None] * inv_freq, col[:, None] * inv_freq], axis=-1)
    cos, sin = jnp.cos(ang), jnp.sin(ang)
    lane = jnp.arange(LANES)
    off = (lane % chunk) - lane0
    in_rot = (off >= 0) & (off < rot_dim)
    first = in_rot & (off < half)
    second = in_rot & (off >= half)
    idx = jnp.clip(off % half, 0, half - 1)
    c_tab = jnp.where(in_rot[None, :], cos[:, idx], 1.0)
    s_hi = jnp.where(first[None, :], -sin[:, idx], 0.0)
    s_lo = jnp.where(second[None, :], sin[:, idx], 0.0)
    lat = jnp.stack([c_tab, s_hi, s_lo])
    ctx = jnp.stack([jnp.ones((CTX, LANES), F32), jnp.zeros((CTX, LANES), F32),
                     jnp.zeros((CTX, LANES), F32)])
    return jnp.concatenate([ctx, lat], axis=1)


def _mla_weight_layouts(w_down, w_uq, w_ukv):
    zeros = functools.partial(jnp.zeros, dtype=w_down.dtype)
    n_lat = MLA_Q_LORA + MLA_KV_LORA
    wd = jnp.concatenate([w_down[:, :n_lat], zeros((D, MLA_NOPE)), w_down[:, n_lat:],
                          zeros((D, LANES - MLA_NOPE - MLA_ROPE))], axis=1)
    wq = w_uq.reshape(MLA_Q_LORA, MLA_HEADS, MLA_NOPE + MLA_ROPE)
    wq = jnp.pad(wq, ((0, 0), (0, 0), (0, LANES - MLA_NOPE - MLA_ROPE))).reshape(MLA_Q_LORA, MLA_Q_W)
    wkv = w_ukv.reshape(MLA_KV_LORA, MLA_HEADS, MLA_NOPE + MLA_V)
    wk = jnp.pad(wkv[..., :MLA_NOPE], ((0, 0), (0, 0), (0, LANES - MLA_NOPE)))
    wkv = jnp.concatenate([wk.reshape(MLA_KV_LORA, MLA_HEADS * LANES),
                           wkv[..., MLA_NOPE:].reshape(MLA_KV_LORA, MLA_HEADS * MLA_V)], axis=1)
    return wd.astype(BF16), wq.astype(BF16), wkv.astype(BF16)


def kernel(x, c, ctx, c_ctx, w_ada, b_ada, norm_g, w_mlp_in, w_mlp_out, w_da_qkv, da_lambda, da_subln, w_da_out, w_mla_down, mla_q_norm, w_mla_uq, mla_kv_norm, w_mla_ukv, w_mla_out, w_sc_in, sc_conv, w_sc_out):
    b = x.shape[0]
    assert x.shape == (b, SEQ, D) and ctx.shape == (b, CTX, D) and b + 1 <= MOD_ROWS

    cc = jnp.concatenate([c, c_ctx[None, :], jnp.zeros((MOD_ROWS - b - 1, D), F32)], axis=0)
    mods = _ada_all_layers(cc, w_ada, b_ada).reshape(DEPTH, MOD_ROWS, N_MOD, D)

    rope_da = _rope_tables(DA_HEAD_DIM, 0, DA_HEAD_DIM)
    rope_mla = _rope_tables(MLA_ROPE, MLA_NOPE, LANES)

    w_da_qkv, w_da_out, w_mla_out, w_sc_in, w_sc_out, w_mlp_in, w_mlp_out = (
        w.astype(BF16) for w in (w_da_qkv, w_da_out, w_mla_out, w_sc_in, w_sc_out, w_mlp_in, w_mlp_out))

    h = (ctx, x)
    for i in range(DEPTH):
        kind, j = i % 3, i // 3
        ctx_out = i < DEPTH - 1
        if kind == 0:
            lambda_init = 0.8 - 0.6 * math.exp(-0.3 * i)
            q, k, v = _da_pre(h, mods, i, norm_g, w_da_qkv, j, rope_da)
            y = _da_attn(q, k, v, da_lambda[j], da_subln[j], lambda_init, ctx_out)
            w_out = w_da_out
        elif kind == 1:
            wd, wq, wkv = _mla_weight_layouts(w_mla_down[j], w_mla_uq[j], w_mla_ukv[j])
            q, k, v = _mla_pre(h, mods, i, norm_g, wd, mla_q_norm[j].reshape(1, -1), wq,
                               mla_kv_norm[j].reshape(1, -1), wkv, rope_mla)
            y = _mla_attn(q, k, v, ctx_out)
            w_out = w_mla_out
        else:
            y = _sc_pre(h, mods, i, norm_g, w_sc_in, sc_conv, j)
            w_out = w_sc_out
        h = _post(h, y, mods, i, norm_g, w_out, j, w_mlp_in, w_mlp_out, ctx_out)
    return h
```

```python
import functools
import math

import jax
import jax.numpy as jnp
from jax import lax
from jax.experimental import pallas as pl
from jax.experimental.pallas import tpu as pltpu

F32 = jnp.float32
BF16 = jnp.bfloat16

LANES = 128
SUBLANES = 8
VMEM_LIMIT_BYTES = 56 * 1024 * 1024

D = 1024
CTX = 256
SEQ = 2048
NT = CTX + SEQ
DEPTH = 4
N_MOD = 6
MLP_HIDDEN = 4 * D
NORM_EPS = 1e-6
ROPE_THETA = 10000.0
GRID_W = 64

DA_HEAD_DIM = 64
DA_HEADS = 8
DA_SCALE = DA_HEAD_DIM ** -0.5

MLA_HEADS = 16
MLA_Q_LORA = 256
MLA_KV_LORA = 128
MLA_NOPE = 64
MLA_ROPE = 32
MLA_V = 64
MLA_SCALE = (MLA_NOPE + MLA_ROPE) ** -0.5
LOG2E = math.log2(math.e)

SC_WIDTH = 3

TM = CTX
NTILES = NT // TM
MLP_CHUNK = 1024
MOD_ROWS = 24
ATTN_PAIRS = 8
STEP_TILES = 3
STEP_ROWS = STEP_TILES * TM


def _params(n_axes):
    return pltpu.CompilerParams(
        dimension_semantics=("arbitrary",) * n_axes,
        vmem_limit_bytes=VMEM_LIMIT_BYTES,
    )


def _rms(x, g):
    ms = jnp.mean(x * x, axis=-1, keepdims=True)
    return x * lax.rsqrt(ms + NORM_EPS) * g


def _modnorm(h, g, mod):
    return _rms(h, g) * (1.0 + mod[1:2, :]) + mod[0:1, :]


def _softmax_pv(scores, v):
    v_ones = jnp.concatenate([v, jnp.ones_like(v)], axis=1)
    es = [jnp.exp2(s - jnp.max(s, axis=-1, keepdims=True)).astype(BF16) for s in scores]
    os_ = [jnp.dot(e, v_ones, preferred_element_type=F32) for e in es]
    return [o[:, :LANES] / o[:, LANES:] for o in os_]


def _pipelined(n, first_stage, second_stage):
    carried = first_stage(0)
    for i in range(1, n):
        upcoming = first_stage(i)
        second_stage(i - 1, carried)
        carried = upcoming
    second_stage(n - 1, carried)


def _rope_block(x, tabs, sh_hi, sh_lo):
    return x * tabs[0] + pltpu.roll(x, sh_hi, 1) * tabs[1] + pltpu.roll(x, sh_lo, 1) * tabs[2]


def _tile_rows(c):
    return slice(c * TM, (c + 1) * TM)


def _rope_tabs(rope_ref, c):
    return [rope_ref[i, _tile_rows(c), :] for i in range(3)]


def _ada_kernel(c_ref, w_ref, b_ref, o_ref):
    c = c_ref[...]
    s = c / (1.0 + jnp.exp(-c))
    o_ref[...] = jnp.dot(s.astype(BF16), w_ref[...].astype(BF16),
                         preferred_element_type=F32) + b_ref[...]


def _ada_all_layers(cc, w_ada, b_ada):
    return pl.pallas_call(
        _ada_kernel,
        grid=(DEPTH, N_MOD),
        in_specs=[
            pl.BlockSpec((MOD_ROWS, D), lambda l, n: (0, 0)),
            pl.BlockSpec((None, D, D), lambda l, n: (l, 0, n)),
            pl.BlockSpec((None, 1, D), lambda l, n: (l, 0, n)),
        ],
        out_specs=pl.BlockSpec((None, MOD_ROWS, D), lambda l, n: (l, 0, n)),
        out_shape=jax.ShapeDtypeStruct((DEPTH, MOD_ROWS, N_MOD * D), F32),
        compiler_params=_params(2),
        name="ada_mod",
    )(cc, w_ada, b_ada.reshape(DEPTH, 1, N_MOD * D))


def _step_spec(width):
    return pl.BlockSpec((None, STEP_ROWS, width), lambda b, j: (b, j, 0))


def _step_mod_specs(layer, batch):
    def spec(c):
        def index(b, j):
            return (layer, jnp.where(STEP_TILES * j + c == 0, batch, b), 0, 0)
        return pl.BlockSpec((None, None, N_MOD, D), index)
    return [spec(c) for c in range(STEP_TILES)]


def _rope_spec():
    return pl.BlockSpec((3, STEP_ROWS, LANES), lambda b, j: (0, j, 0))


def _full_spec(shape):
    nd = len(shape)
    return pl.BlockSpec(shape, lambda b, t: (0,) * nd, pipeline_mode=pl.Buffered(1))


def _modnorm_tiles(h_ref, g, mod_refs):
    return [_modnorm(h_ref[_tile_rows(c), :], g, mod_refs[c][...]).astype(BF16)
            for c in range(STEP_TILES)]


def _layer_spec(shape, layer):
    nd = len(shape)
    return pl.BlockSpec((None,) + shape, lambda b, t: (layer,) + (0,) * nd,
                        pipeline_mode=pl.Buffered(1))


def _split_stream_specs(tiles, first_tile):
    def lat_spec(c):
        return pl.BlockSpec(
            (None, TM, D), lambda b, j: (b, jnp.maximum(tiles * j + c + first_tile - 1, 0), 0))
    return [pl.BlockSpec((None, TM, D), lambda b, j: (b, 0, 0))] + [lat_spec(c) for c in range(tiles)]


def _split_stream_tiles(ctx_ref, lat_refs):
    first = jnp.where(pl.program_id(1) == 0, ctx_ref[...], lat_refs[0][...])
    return [first] + [r[...] for r in lat_refs[1:]]


def _da_pre_kernel(*refs, split):
    n_in = 1 + STEP_TILES if split else 1
    mod_refs = refs[n_in:n_in + STEP_TILES]
    g_ref, w_ref, rope_ref, q_ref, k_ref, v_ref = refs[n_in + STEP_TILES:]
    if split:
        h = _split_stream_tiles(refs[0], refs[1:n_in])
        a = [_modnorm(h[c], g_ref[0:1, :], mod_refs[c][...]).astype(BF16) for c in range(STEP_TILES)]
    else:
        a = _modnorm_tiles(refs[0], g_ref[0:1, :], mod_refs)
    qkv = [jnp.dot(a_c, w_ref[...], preferred_element_type=F32) for a_c in a]
    for c in range(STEP_TILES):
        rows, tabs = _tile_rows(c), _rope_tabs(rope_ref, c)
        for j in range(D // LANES):
            lo, hi = j * LANES, (j + 1) * LANES
            q = _rope_block(qkv[c][:, lo:hi], tabs, LANES - 32, 32)
            q_ref[rows, lo:hi] = (q * (DA_SCALE * LOG2E)).astype(BF16)
            k = _rope_block(qkv[c][:, D + lo:D + hi], tabs, LANES - 32, 32)
            k_ref[rows, lo:hi] = k.astype(BF16)
        v_ref[rows, :] = qkv[c][:, 2 * D:].astype(BF16)


def _da_pre(h, mods, layer, norm_g, w_qkv, j, rope):
    split = isinstance(h, tuple)
    streams = (h[0],) + (h[1],) * STEP_TILES if split else (h,)
    b = streams[0].shape[0]
    out = jax.ShapeDtypeStruct((b, NT, D), BF16)
    return pl.pallas_call(
        functools.partial(_da_pre_kernel, split=split),
        grid=(b, NT // STEP_ROWS),
        in_specs=(_split_stream_specs(STEP_TILES, 0) if split else [_step_spec(D)])
                 + _step_mod_specs(layer, b)
                 + [_layer_spec((4, D), layer), _layer_spec((D, 3 * D), j), _rope_spec()],
        out_specs=[_step_spec(D)] * 3,
        out_shape=[out] * 3,
        compiler_params=_params(2),
        name="da_pre",
    )(*streams, *([mods] * STEP_TILES), norm_g, w_qkv, rope)


def _da_attn_kernel(lam_ref, sub_ref, q_ref, k_ref, v_ref, o_ref, *, lambda_init, first_tile):
    lv = lam_ref[...]
    lam = (jnp.exp(jnp.sum(lv[0:1] * lv[1:2], axis=-1, keepdims=True))
           - jnp.exp(jnp.sum(lv[2:3] * lv[3:4], axis=-1, keepdims=True))
           + lambda_init)

    def attend(nk):
        def scores(p):
            lo, hi = p * LANES, (p + 1) * LANES
            q = q_ref[:, lo:hi]
            lane = lax.broadcasted_iota(jnp.int32, q.shape, 1)
            zero = jnp.zeros_like(q)
            return [lax.dot_general(qm, k_ref[0:nk, lo:hi], (((1,), (1,)), ((), ())),
                                    preferred_element_type=F32)
                    for qm in (jnp.where(lane < DA_HEAD_DIM, q, zero),
                               jnp.where(lane >= DA_HEAD_DIM, q, zero))]

        def finish(p, s):
            lo, hi = p * LANES, (p + 1) * LANES
            pv = _softmax_pv(s, v_ref[0:nk, lo:hi])
            o = pv[0] - lam * pv[1]
            o_ref[:, lo:hi] = (_rms(o, sub_ref[...]) * (1.0 - lambda_init)).astype(BF16)

        _pipelined(ATTN_PAIRS, scores, finish)

    if first_tile == 0:
        t = pl.program_id(2)
        pl.when(t == 0)(lambda: attend(CTX))
        pl.when(t > 0)(lambda: attend(NT))
    else:
        attend(NT)


def _da_attn(q, k, v, lam_vecs, subln, lambda_init, ctx_out):
    b = q.shape[0]
    first_tile = 0 if ctx_out else 1
    width = ATTN_PAIRS * LANES
    kv_spec = pl.BlockSpec((None, NT, width), lambda b_, h, t: (b_, 0, h))
    n_tiles = NTILES - first_tile
    return pl.pallas_call(
        functools.partial(_da_attn_kernel, lambda_init=lambda_init, first_tile=first_tile),
        grid=(b, DA_HEADS // ATTN_PAIRS, n_tiles),
        in_specs=[
            pl.BlockSpec((4, DA_HEAD_DIM), lambda b_, h, t: (0, 0)),
            pl.BlockSpec((1, LANES), lambda b_, h, t: (0, 0)),
            pl.BlockSpec((None, TM, width), lambda b_, h, t: (b_, t + first_tile, h)),
            kv_spec, kv_spec,
        ],
        out_specs=pl.BlockSpec((None, TM, width), lambda b_, h, t: (b_, t, h)),
        out_shape=jax.ShapeDtypeStruct((b, n_tiles * TM, D), BF16),
        compiler_params=_params(3),
        name="da_attn",
    )(lam_vecs, subln.reshape(1, LANES), q, k, v)


MLA_DOWN_W = 512
MLA_Q_W = MLA_HEADS * LANES
MLA_KV_W = MLA_HEADS * LANES + MLA_HEADS * MLA_V


def _mla_pre_kernel(*refs):
    h_ref, mod_refs = refs[0], refs[1:1 + STEP_TILES]
    (g_ref, wd_ref, qg_ref, wq_ref, kg_ref, wkv_ref, rope_ref,
     q_ref, k_ref, v_ref) = refs[1 + STEP_TILES:]
    tiles = range(STEP_TILES)
    a = _modnorm_tiles(h_ref, g_ref[0:1, :], mod_refs)
    d = [jnp.dot(a_c, wd_ref[...], preferred_element_type=F32) for a_c in a]
    cq = [_rms(d_c[:, :MLA_Q_LORA], qg_ref[...]).astype(BF16) for d_c in d]
    ckv = [_rms(d_c[:, MLA_Q_LORA:MLA_Q_LORA + MLA_KV_LORA], kg_ref[...]).astype(BF16) for d_c in d]
    q = [jnp.dot(cq[c], wq_ref[...], preferred_element_type=F32) for c in tiles]
    kv = [jnp.dot(ckv[c], wkv_ref[...], preferred_element_type=F32) for c in tiles]
    for c in tiles:
        rows, tabs = _tile_rows(c), _rope_tabs(rope_ref, c)
        kr = _rope_block(d[c][:, MLA_Q_LORA + MLA_KV_LORA:], tabs, LANES - 16, 16)
        for j in range(MLA_HEADS):
            lo, hi = j * LANES, (j + 1) * LANES
            qj = _rope_block(q[c][:, lo:hi], tabs, LANES - 16, 16)
            q_ref[rows, lo:hi] = (qj * (MLA_SCALE * LOG2E)).astype(BF16)
            k_ref[rows, lo:hi] = (kv[c][:, lo:hi] + kr).astype(BF16)
        v_ref[rows, :] = kv[c][:, MLA_HEADS * LANES:].astype(BF16)


def _mla_pre(h, mods, layer, norm_g, wd, qg, wq, kg, wkv, rope):
    b = h.shape[0]
    return pl.pallas_call(
        _mla_pre_kernel,
        grid=(b, NT // STEP_ROWS),
        in_specs=[_step_spec(D)] + _step_mod_specs(layer, b) + [
            _layer_spec((4, D), layer),
            _full_spec((D, MLA_DOWN_W)),
            _full_spec((1, MLA_Q_LORA)),
            _full_spec((MLA_Q_LORA, MLA_Q_W)),
            _full_spec((1, MLA_KV_LORA)),
            _full_spec((MLA_KV_LORA, MLA_KV_W)),
            _rope_spec(),
        ],
        out_specs=[_step_spec(MLA_Q_W), _step_spec(MLA_Q_W), _step_spec(D)],
        out_shape=[jax.ShapeDtypeStruct((b, NT, MLA_Q_W), BF16),
                   jax.ShapeDtypeStruct((b, NT, MLA_Q_W), BF16),
                   jax.ShapeDtypeStruct((b, NT, D), BF16)],
        compiler_params=_params(2),
        name="mla_pre",
    )(h, *([mods] * STEP_TILES), norm_g, wd, qg, wq, kg, wkv, rope)


def _mla_attn_kernel(q_ref, k_ref, v_ref, o_ref, *, first_tile):
    def attend(nk):
        def scores(p):
            return [lax.dot_general(q_ref[:, h * LANES:(h + 1) * LANES],
                                    k_ref[0:nk, h * LANES:(h + 1) * LANES],
                                    (((1,), (1,)), ((), ())), preferred_element_type=F32)
                    for h in (2 * p, 2 * p + 1)]

        def finish(p, s):
            pv = _softmax_pv(s, v_ref[0:nk, p * LANES:(p + 1) * LANES])
            lane = lax.broadcasted_iota(jnp.int32, (TM, LANES), 1)
            o_ref[:, p * LANES:(p + 1) * LANES] = jnp.where(lane < MLA_V, pv[0], pv[1]).astype(BF16)

        _pipelined(ATTN_PAIRS, scores, finish)

    if first_tile == 0:
        t = pl.program_id(2)
        pl.when(t == 0)(lambda: attend(CTX))
        pl.when(t > 0)(lambda: attend(NT))
    else:
        attend(NT)


def _mla_attn(q, k, v, ctx_out):
    b = q.shape[0]
    first_tile = 0 if ctx_out else 1
    n_tiles = NTILES - first_tile
    steps = MLA_HEADS // (2 * ATTN_PAIRS)
    width = ATTN_PAIRS * LANES
    return pl.pallas_call(
        functools.partial(_mla_attn_kernel, first_tile=first_tile),
        grid=(b, steps, n_tiles),
        in_specs=[
            pl.BlockSpec((None, TM, 2 * width), lambda b_, h, t: (b_, t + first_tile, h)),
            pl.BlockSpec((None, NT, 2 * width), lambda b_, h, t: (b_, 0, h)),
            pl.BlockSpec((None, NT, width), lambda b_, h, t: (b_, 0, h)),
        ],
        out_specs=pl.BlockSpec((None, TM, width), lambda b_, h, t: (b_, t, h)),
        out_shape=jax.ShapeDtypeStruct((b, n_tiles * TM, D), BF16),
        compiler_params=_params(3),
        name="mla_attn",
    )(q, k, v)


def _sc_pre_kernel(*refs):
    hp_ref, h_ref, hn_ref, mod_refs = refs[0], refs[1], refs[2], refs[3:3 + STEP_TILES]
    g_ref, w_ref, cw_ref, y_ref = refs[3 + STEP_TILES:]
    step = pl.program_id(1)
    tiles = range(STEP_TILES)
    halo = SUBLANES
    rows = jnp.concatenate([hp_ref[...], h_ref[...], hn_ref[...]], axis=0)
    a = [_modnorm(rows[c * TM:(c + 1) * TM + 2 * halo], g_ref[0:1, :], mod_refs[c][...]).astype(BF16)
         for c in tiles]
    p = [jnp.dot(a_c, w_ref[...], preferred_element_type=F32) for a_c in a]
    row = lax.broadcasted_iota(jnp.int32, (TM, 1), 0)
    for c in tiles:
        tile = STEP_TILES * step + c
        z = p[c][:, D:2 * D] * p[c][:, 2 * D:]
        b_gate = p[c][halo:halo + TM, :D]
        first = jnp.logical_and(row == 0, tile <= 1)
        last = jnp.logical_and(row == TM - 1, jnp.logical_or(tile == 0, tile == NTILES - 1))
        z_prev = jnp.where(first, 0.0, z[halo - 1:halo - 1 + TM])
        z_next = jnp.where(last, 0.0, z[halo + 1:halo + 1 + TM])
        u = cw_ref[0:1, :] * z_prev + cw_ref[1:2, :] * z[halo:halo + TM] + cw_ref[2:3, :] * z_next
        y_ref[_tile_rows(c), :] = (b_gate * u).astype(BF16)


def _sc_pre(h, mods, layer, norm_g, w_in, conv_w, j):
    b = h.shape[0]
    per_step = STEP_ROWS // SUBLANES
    last_halo = NT // SUBLANES - 1
    return pl.pallas_call(
        _sc_pre_kernel,
        grid=(b, NT // STEP_ROWS),
        in_specs=[
            pl.BlockSpec((None, SUBLANES, D),
                         lambda b_, j: (b_, jnp.maximum(j * per_step - 1, 0), 0)),
            _step_spec(D),
            pl.BlockSpec((None, SUBLANES, D),
                         lambda b_, j: (b_, jnp.minimum((j + 1) * per_step, last_halo), 0)),
        ] + _step_mod_specs(layer, b) + [
            _layer_spec((4, D), layer),
            _layer_spec((D, 3 * D), j),
            _layer_spec((SC_WIDTH, D), j),
        ],
        out_specs=_step_spec(D),
        out_shape=jax.ShapeDtypeStruct((b, NT, D), BF16),
        compiler_params=_params(2),
        name="sc_pre",
    )(h, h, h, *([mods] * STEP_TILES), norm_g, w_in, conv_w)


def _post_kernel(*refs, tiles, split):
    if split:
        h_in, refs = _split_stream_tiles(refs[0], refs[1:1 + tiles]), refs[1 + tiles:]
    else:
        h_in, refs = [ref[...] for ref in refs[:tiles]], refs[tiles:]
    y_ref, mod_refs = refs[0], refs[1:1 + tiles]
    g_ref, wo_ref, w1_ref, w2_ref, o_ref = refs[1 + tiles:]
    r = range(tiles)
    mod = [mod_refs[c][...] for c in r]
    y = [jnp.dot(y_ref[_tile_rows(c), :], wo_ref[...], preferred_element_type=F32) for c in r]
    h = [h_in[c] + mod[c][2:3, :] * _rms(y[c], g_ref[1:2, :]) for c in r]
    a = [(_rms(h[c], g_ref[2:3, :]) * (1.0 + mod[c][4:5, :]) + mod[c][3:4, :]).astype(BF16) for c in r]
    f = [jnp.zeros((TM, D), F32) for c in r]
    for m in range(MLP_HIDDEN // MLP_CHUNK):
        lo, hi = m * MLP_CHUNK, (m + 1) * MLP_CHUNK
        u = [jnp.maximum(jnp.dot(a[c], w1_ref[:, lo:hi], preferred_element_type=F32), 0.0) for c in r]
        f = [f[c] + jnp.dot((u[c] * u[c]).astype(BF16), w2_ref[lo:hi, :], preferred_element_type=F32)
             for c in r]
    for c in r:
        o_ref[_tile_rows(c), :] = h[c] + mod[c][5:6, :] * _rms(f[c], g_ref[3:4, :])


def _post(h, y, mods, layer, norm_g, w_out, j, w1, w2, ctx_out):
    split = isinstance(h, tuple)
    b = y.shape[0]
    first_tile = 0 if ctx_out else 1
    n_tiles = NTILES - first_tile
    tiles = 3 if n_tiles % 3 == 0 else 2
    assert n_tiles % tiles == 0

    def tile_spec(c):
        return pl.BlockSpec((None, TM, D), lambda b_, s: (b_, tiles * s + c + first_tile, 0))

    def mod_spec(c):
        def index(b_, s):
            return (layer, jnp.where(tiles * s + c + first_tile == 0, b, b_), 0, 0)
        return pl.BlockSpec((None, None, N_MOD, D), index)

    chain = range(tiles)
    streams = (h[0],) + (h[1],) * tiles if split else (h,) * tiles
    assert y.shape == (b, n_tiles * TM, D)
    step_spec = pl.BlockSpec((None, tiles * TM, D), lambda b_, s: (b_, s, 0))
    return pl.pallas_call(
        functools.partial(_post_kernel, tiles=tiles, split=split),
        grid=(b, n_tiles // tiles),
        in_specs=(_split_stream_specs(tiles, first_tile) if split else [tile_spec(c) for c in chain])
                 + [step_spec] + [mod_spec(c) for c in chain]
                 + [_layer_spec((4, D), layer), _layer_spec((D, D), j),
                    _layer_spec((D, MLP_HIDDEN), layer), _layer_spec((MLP_HIDDEN, D), layer)],
        out_specs=step_spec,
        out_shape=jax.ShapeDtypeStruct((b, n_tiles * TM, D), F32),
        compiler_params=_params(2),
        name="post_mlp",
    )(*streams, y, *([mods] * tiles), norm_g, w_out, w1, w2)


def _rope_tables(rot_dim, lane0, chunk):
    quarter = rot_dim // 4
    half = rot_dim // 2
    pos = jnp.arange(SEQ, dtype=F32)
    row = jnp.floor(pos / GRID_W)
    col = pos - row * GRID_W
    inv_freq = ROPE_THETA ** (-jnp.arange(quarter, dtype=F32) / quarter)
    ang = jnp.concatenate([row[:, None] * inv_freq, col[:, None] * inv_freq], axis=-1)
    cos, sin = jnp.cos(ang), jnp.sin(ang)
    lane = jnp.arange(LANES)
    off = (lane % chunk) - lane0
    in_rot = (off >= 0) & (off < rot_dim)
    first = in_rot & (off < half)
    second = in_rot & (off >= half)
    idx = jnp.clip(off % half, 0, half - 1)
    c_tab = jnp.where(in_rot[None, :], cos[:, idx], 1.0)
    s_hi = jnp.where(first[None, :], -sin[:, idx], 0.0)
    s_lo = jnp.where(second[None, :], sin[:, idx], 0.0)
    lat = jnp.stack([c_tab, s_hi, s_lo])
    ctx = jnp.stack([jnp.ones((CTX, LANES), F32), jnp.zeros((CTX, LANES), F32),
                     jnp.zeros((CTX, LANES), F32)])
    return jnp.concatenate([ctx, lat], axis=1)


def _mla_weight_layouts(w_down, w_uq, w_ukv):
    zeros = functools.partial(jnp.zeros, dtype=w_down.dtype)
    n_lat = MLA_Q_LORA + MLA_KV_LORA
    wd = jnp.concatenate([w_down[:, :n_lat], zeros((D, MLA_NOPE)), w_down[:, n_lat:],
                          zeros((D, LANES - MLA_NOPE - MLA_ROPE))], axis=1)
    wq = w_uq.reshape(MLA_Q_LORA, MLA_HEADS, MLA_NOPE + MLA_ROPE)
    wq = jnp.pad(wq, ((0, 0), (0, 0), (0, LANES - MLA_NOPE - MLA_ROPE))).reshape(MLA_Q_LORA, MLA_Q_W)
    wkv = w_ukv.reshape(MLA_KV_LORA, MLA_HEADS, MLA_NOPE + MLA_V)
    wk = jnp.pad(wkv[..., :MLA_NOPE], ((0, 0), (0, 0), (0, LANES - MLA_NOPE)))
    wkv = jnp.concatenate([wk.reshape(MLA_KV_LORA, MLA_HEADS * LANES),
                           wkv[..., MLA_NOPE:].reshape(MLA_KV_LORA, MLA_HEADS * MLA_V)], axis=1)
    return wd.astype(BF16), wq.astype(BF16), wkv.astype(BF16)


def kernel(x, c, ctx, c_ctx, w_ada, b_ada, norm_g, w_mlp_in, w_mlp_out, w_da_qkv, da_lambda, da_subln, w_da_out, w_mla_down, mla_q_norm, w_mla_uq, mla_kv_norm, w_mla_ukv, w_mla_out, w_sc_in, sc_conv, w_sc_out):
    b = x.shape[0]
    assert x.shape == (b, SEQ, D) and ctx.shape == (b, CTX, D) and b + 1 <= MOD_ROWS

    cc = jnp.concatenate([c, c_ctx[None, :], jnp.zeros((MOD_ROWS - b - 1, D), F32)], axis=0)
    mods = _ada_all_layers(cc, w_ada, b_ada).reshape(DEPTH, MOD_ROWS, N_MOD, D)

    rope_da = _rope_tables(DA_HEAD_DIM, 0, DA_HEAD_DIM)
    rope_mla = _rope_tables(MLA_ROPE, MLA_NOPE, LANES)

    w_da_qkv, w_da_out, w_mla_out, w_sc_in, w_sc_out, w_mlp_in, w_mlp_out = (
        w.astype(BF16) for w in (w_da_qkv, w_da_out, w_mla_out, w_sc_in, w_sc_out, w_mlp_in, w_mlp_out))

    h = (ctx, x)
    for i in range(DEPTH):
        kind, j = i % 3, i // 3
        ctx_out = i < DEPTH - 1
        if kind == 0:
            lambda_init = 0.8 - 0.6 * math.exp(-0.3 * i)
            q, k, v = _da_pre(h, mods, i, norm_g, w_da_qkv, j, rope_da)
            y = _da_attn(q, k, v, da_lambda[j], da_subln[j], lambda_init, ctx_out)
            w_out = w_da_out
        elif kind == 1:
            wd, wq, wkv = _mla_weight_layouts(w_mla_down[j], w_mla_uq[j], w_mla_ukv[j])
            q, k, v = _mla_pre(h, mods, i, norm_g, wd, mla_q_norm[j].reshape(1, -1), wq,
                               mla_kv_norm[j].reshape(1, -1), wkv, rope_mla)
            y = _mla_attn(q, k, v, ctx_out)
            w_out = w_mla_out
        else:
            y = _sc_pre(h, mods, i, norm_g, w_sc_in, sc_conv, j)
            w_out = w_sc_out
        h = _post(h, y, mods, i, norm_g, w_out, j, w_mlp_in, w_mlp_out, ctx_out)
    return h
```
